```python
import jax, jax.numpy as jnp
from jax import lax
import numpy as np

D_MODEL = 1024
BATCH = 4
SEQ = 4096
DEPTH = 4
DEC_BATCH = 32
DEC_SEQ = 1
PAST_LEN = 8192
PAGE_SIZE = 128

HEAD_DIM = 64
HEADS_PER_GROUP = D_MODEL // HEAD_DIM
WINDOWS = (128, 512, 2048)
DILATIONS = (1, 4, 16)
N_GROUPS = len(WINDOWS)
SPAN = WINDOWS[0] // DILATIONS[0]
BLOCK = SPAN
N_HEADS_TOTAL = N_GROUPS * HEADS_PER_GROUP
N_BUCKETS = 32
MAX_DISTANCE = WINDOWS[-1]
POOL_WINDOWS = (2, 4, 8, 16)
POOL_GROUP_WIDTH = D_MODEL // len(POOL_WINDOWS)
POOL_STATE = max(POOL_WINDOWS) - 1
D_FF = 11 * D_MODEL // 4
CONV_WIDTH = 3
N_ATTN_LAYERS = (DEPTH + 1) // 2
N_POOL_LAYERS = DEPTH // 2
EPS = 1e-6
NEG_INF = -1e30
SCALE = HEAD_DIM ** -0.5

kernel_name = "dilated_attn_pool_convglu_hybrid_step"


def rmsnorm(x, g):
    xf = x.astype(jnp.float32)
    y = xf * lax.rsqrt(jnp.mean(xf * xf, axis=-1, keepdims=True) + EPS)
    return (y * g.astype(jnp.float32)).astype(x.dtype)


def t5_buckets(dist):
    max_exact = N_BUCKETS // 2
    n = np.maximum(dist, 1).astype(np.float32)
    large = max_exact + (np.log(n / max_exact) / np.log(MAX_DISTANCE / max_exact)
                         * (N_BUCKETS - max_exact)).astype(np.int32)
    large = np.minimum(large, N_BUCKETS - 1)
    return np.where(dist < max_exact, dist, large).astype(np.int32)


def group_bias(rel_bias, g):
    dist = np.arange(SPAN + 1) * DILATIONS[g]
    b = rel_bias[t5_buckets(dist)]
    return b[:, g * HEADS_PER_GROUP:(g + 1) * HEADS_PER_GROUP].T.astype(jnp.float32)


def banded_attention(q, k, v, bias):
    n, L, H, hd = q.shape
    nb = L // BLOCK
    qb = q.reshape(n, nb, BLOCK, H, hd)

    def band(a):
        a = jnp.concatenate([jnp.zeros((n, BLOCK, H, hd), a.dtype), a], axis=1)
        a = a.reshape(n, nb + 1, BLOCK, H, hd)
        return jnp.concatenate([a[:, :-1], a[:, 1:]], axis=2)

    kb, vb = band(k), band(v)
    qi = np.arange(BLOCK)[:, None]
    kj = np.arange(2 * BLOCK)[None, :]
    rel = qi - kj + BLOCK
    key_pos = np.arange(nb)[:, None, None] * BLOCK - BLOCK + kj[None]
    valid = (rel >= 0) & (rel <= SPAN) & (key_pos >= 0)
    s = jnp.einsum("nbqhd,nbkhd->nbhqk", qb, kb).astype(jnp.float32) * SCALE
    s = s + bias[:, np.clip(rel, 0, SPAN)]
    s = jnp.where(valid[None, :, None], s, NEG_INF)
    m = jnp.max(s, axis=-1)
    p = jnp.exp(s - m[..., None])
    l = jnp.sum(p, axis=-1)
    o = jnp.einsum("nbhqk,nbkhd->nbqhd", p.astype(v.dtype), vb).astype(jnp.float32)
    m = jnp.swapaxes(m, 2, 3)
    l = jnp.swapaxes(l, 2, 3)
    o = o / l[..., None]
    return o.reshape(n, L, H, hd), m.reshape(n, L, H), l.reshape(n, L, H)


def dilated_prompt(q, k, v, dil, bias):
    B, T, H, hd = q.shape
    tc = T // dil
    lp = -(-tc // BLOCK) * BLOCK

    def to_stream(a):
        a = a.reshape(B, tc, dil, H, hd).transpose(0, 2, 1, 3, 4).reshape(B * dil, tc, H, hd)
        return jnp.pad(a, ((0, 0), (0, lp - tc), (0, 0), (0, 0)))

    def from_stream(a):
        a = a[:, :tc].reshape((B, dil, tc) + a.shape[2:])
        return jnp.swapaxes(a, 1, 2).reshape((B, T) + a.shape[3:])

    o, m, l = banded_attention(to_stream(q), to_stream(k), to_stream(v), bias)
    return from_stream(o), from_stream(m), from_stream(l)


def dilated_sample(q, k_new, v_new, k_cache, v_cache, dil, bias):
    lw = k_cache.shape[1]
    S = q.shape[1]
    kc = jnp.concatenate([k_cache, k_new], axis=1)
    vc = jnp.concatenate([v_cache, v_new], axis=1)
    idx = lw + np.arange(S)[:, None] - np.arange(SPAN + 1)[None, :] * dil
    valid = idx >= 0
    idx_c = np.maximum(idx, 0)
    kg = kc[:, idx_c]
    vg = vc[:, idx_c]
    s = jnp.einsum("bshd,bsjhd->bshj", q, kg).astype(jnp.float32) * SCALE + bias
    s = jnp.where(valid[None, :, None, :], s, NEG_INF)
    m = jnp.max(s, axis=-1)
    p = jnp.exp(s - m[..., None])
    l = jnp.sum(p, axis=-1)
    o = jnp.einsum("bshj,bsjhd->bshd", p.astype(vg.dtype), vg).astype(jnp.float32) / l[..., None]
    return o, m, l


def merge_groups(outs):
    o = jnp.stack([t[0] for t in outs])
    m = jnp.stack([t[1] for t in outs])
    l = jnp.stack([t[2] for t in outs])
    w = l * jnp.exp(m - jnp.max(m, axis=0))
    w = w / jnp.sum(w, axis=0)
    return jnp.einsum("gbthd,gbth->bthd", o, w)


def pool_mix(h_ext, n_prefix, w_pool, scale):
    hf = h_ext.astype(jnp.float32)
    c = jnp.pad(jnp.cumsum(hf, axis=1), ((0, 0), (1, 0), (0, 0)))
    r = np.arange(n_prefix, h_ext.shape[1])
    outs = []
    for g, w in enumerate(POOL_WINDOWS):
        sl = slice(g * POOL_GROUP_WIDTH, (g + 1) * POOL_GROUP_WIDTH)
        lo = np.maximum(r + 1 - w, 0)
        cnt = (r + 1 - lo).astype(np.float32)
        mean = (c[:, r + 1, sl] - c[:, lo, sl]) / cnt[None, :, None]
        z = (mean - hf[:, r, sl]).astype(h_ext.dtype)
        outs.append(jnp.einsum("bsc,ce->bse", z, w_pool[g]))
    return jnp.concatenate(outs, axis=-1) * scale


def conv_glu(h, gate_prefix, w_in, conv_w, conv_b, w_out):
    u = h @ w_in
    gate, val = u[..., :D_FF], u[..., D_FF:]
    ext = jnp.concatenate([gate_prefix, gate], axis=1)
    S = h.shape[1]
    conv = conv_b + ext[:, 0:S] * conv_w[0]
    for j in range(1, CONV_WIDTH):
        conv = conv + ext[:, j:j + S] * conv_w[j]
    y = (jax.nn.silu(conv) * val) @ w_out
    return y, ext[:, ext.shape[1] - (CONV_WIDTH - 1):]


def setup_inputs(seed: int = 0) -> dict:
    key = jax.random.key(seed)
    ks = jax.random.split(key, 24)
    f32 = jnp.float32

    def nrm(k, shape, s=1.0):
        return jax.random.normal(k, shape, f32) * s

    def kv_shape(w):
        return (N_ATTN_LAYERS, DEC_BATCH, min(w, PAST_LEN), HEADS_PER_GROUP, HEAD_DIM)

    return {
        "x_prompt": nrm(ks[0], (BATCH, SEQ, D_MODEL)),
        "x_sample": nrm(ks[1], (DEC_BATCH, DEC_SEQ, D_MODEL)),
        "cache_k_w128": nrm(ks[2], kv_shape(WINDOWS[0])),
        "cache_v_w128": nrm(ks[3], kv_shape(WINDOWS[0])),
        "cache_k_w512": nrm(ks[4], kv_shape(WINDOWS[1])),
        "cache_v_w512": nrm(ks[5], kv_shape(WINDOWS[1])),
        "cache_k_w2048": nrm(ks[6], kv_shape(WINDOWS[2])),
        "cache_v_w2048": nrm(ks[7], kv_shape(WINDOWS[2])),
        "state_pool": nrm(ks[8], (N_POOL_LAYERS, DEC_BATCH, POOL_STATE, D_MODEL)),
        "state_conv": nrm(ks[9], (DEPTH, DEC_BATCH, CONV_WIDTH - 1, D_FF)),
        "rel_bias": nrm(ks[10], (N_BUCKETS, N_HEADS_TOTAL), 0.2),
        "norm_mix": 1.0 + nrm(ks[11], (DEPTH, D_MODEL), 0.02),
        "w_qkv": nrm(ks[12], (N_ATTN_LAYERS, D_MODEL, N_GROUPS * 3 * HEADS_PER_GROUP * HEAD_DIM), D_MODEL ** -0.5),
        "w_o": nrm(ks[13], (N_ATTN_LAYERS, HEADS_PER_GROUP * HEAD_DIM, D_MODEL), D_MODEL ** -0.5),
        "w_pool": nrm(ks[14], (N_POOL_LAYERS, len(POOL_WINDOWS), POOL_GROUP_WIDTH, POOL_GROUP_WIDTH), POOL_GROUP_WIDTH ** -0.5),
        "pool_scale": 1.0 + nrm(ks[15], (N_POOL_LAYERS, D_MODEL), 0.02),
        "norm_ffn": 1.0 + nrm(ks[16], (DEPTH, D_MODEL), 0.02),
        "w_in": nrm(ks[17], (DEPTH, D_MODEL, 2 * D_FF), D_MODEL ** -0.5),
        "conv_w": nrm(ks[18], (DEPTH, CONV_WIDTH, D_FF), CONV_WIDTH ** -0.5),
        "conv_b": nrm(ks[19], (DEPTH, D_FF), 0.01),
        "w_out": nrm(ks[20], (DEPTH, D_FF, D_MODEL), D_FF ** -0.5),
        "norm_final": 1.0 + nrm(ks[21], (D_MODEL,), 0.02),
    }


def reference(x_prompt, x_sample, cache_k_w128, cache_v_w128, cache_k_w512, cache_v_w512,
              cache_k_w2048, cache_v_w2048, state_pool, state_conv, rel_bias, norm_mix,
              w_qkv, w_o, w_pool, pool_scale, norm_ffn, w_in, conv_w, conv_b, w_out, norm_final):
    cache_k = (cache_k_w128, cache_k_w512, cache_k_w2048)
    cache_v = (cache_v_w128, cache_v_w512, cache_v_w2048)
    biases = [group_bias(rel_bias, g) for g in range(N_GROUPS)]
    bp, T, _ = x_prompt.shape
    bs, S, _ = x_sample.shape
    nk_p = [[] for _ in range(N_GROUPS)]
    nv_p = [[] for _ in range(N_GROUPS)]
    nk_s = [[] for _ in range(N_GROUPS)]
    nv_s = [[] for _ in range(N_GROUPS)]
    pool_p, pool_s, conv_p, conv_s = [], [], [], []
    xp, xs = x_prompt, x_sample
    for i in range(DEPTH):
        if i % 2 == 0:
            a = i // 2
            qkv_p = (rmsnorm(xp, norm_mix[i]) @ w_qkv[a]).reshape(bp, T, N_GROUPS, 3, HEADS_PER_GROUP, HEAD_DIM)
            qkv_s = (rmsnorm(xs, norm_mix[i]) @ w_qkv[a]).reshape(bs, S, N_GROUPS, 3, HEADS_PER_GROUP, HEAD_DIM)
            outs_p, outs_s = [], []
            for g in range(N_GROUPS):
                qp, kp, vp = qkv_p[:, :, g, 0], qkv_p[:, :, g, 1], qkv_p[:, :, g, 2]
                qs, ksn, vsn = qkv_s[:, :, g, 0], qkv_s[:, :, g, 1], qkv_s[:, :, g, 2]
                outs_p.append(dilated_prompt(qp, kp, vp, DILATIONS[g], biases[g]))
                outs_s.append(dilated_sample(qs, ksn, vsn, cache_k[g][a], cache_v[g][a], DILATIONS[g], biases[g]))
                keep = min(WINDOWS[g], T)
                nk_p[g].append(kp[:, T - keep:])
                nv_p[g].append(vp[:, T - keep:])
                nk_s[g].append(ksn)
                nv_s[g].append(vsn)
            xp = xp + merge_groups(outs_p).astype(xp.dtype).reshape(bp, T, D_MODEL) @ w_o[a]
            xs = xs + merge_groups(outs_s).astype(xs.dtype).reshape(bs, S, D_MODEL) @ w_o[a]
        else:
            b = i // 2
            hp = rmsnorm(xp, norm_mix[i])
            hs = rmsnorm(xs, norm_mix[i])
            hs_ext = jnp.concatenate([state_pool[b].astype(hs.dtype), hs], axis=1)
            xp = xp + pool_mix(hp, 0, w_pool[b], pool_scale[b])
            xs = xs + pool_mix(hs_ext, POOL_STATE, w_pool[b], pool_scale[b])
            pool_p.append(hp[:, T - POOL_STATE:])
            pool_s.append(hs_ext[:, hs_ext.shape[1] - POOL_STATE:])
        hp = rmsnorm(xp, norm_ffn[i])
        hs = rmsnorm(xs, norm_ffn[i])
        yp_i, cp = conv_glu(hp, jnp.zeros((bp, CONV_WIDTH - 1, D_FF), hp.dtype), w_in[i], conv_w[i], conv_b[i], w_out[i])
        ys_i, cs = conv_glu(hs, state_conv[i].astype(hs.dtype), w_in[i], conv_w[i], conv_b[i], w_out[i])
        xp = xp + yp_i
        xs = xs + ys_i
        conv_p.append(cp)
        conv_s.append(cs)
    y_prompt = rmsnorm(xp, norm_final)
    y_sample = rmsnorm(xs, norm_final)
    new_k_w128_prompt = jnp.stack(nk_p[0])
    new_v_w128_prompt = jnp.stack(nv_p[0])
    new_k_w512_prompt = jnp.stack(nk_p[1])
    new_v_w512_prompt = jnp.stack(nv_p[1])
    new_k_w2048_prompt = jnp.stack(nk_p[2])
    new_v_w2048_prompt = jnp.stack(nv_p[2])
    new_k_w128_sample = jnp.stack(nk_s[0])
    new_v_w128_sample = jnp.stack(nv_s[0])
    new_k_w512_sample = jnp.stack(nk_s[1])
    new_v_w512_sample = jnp.stack(nv_s[1])
    new_k_w2048_sample = jnp.stack(nk_s[2])
    new_v_w2048_sample = jnp.stack(nv_s[2])
    new_state_pool_prompt = jnp.stack(pool_p)
    new_state_pool_sample = jnp.stack(pool_s)
    new_state_conv_prompt = jnp.stack(conv_p)
    new_state_conv_sample = jnp.stack(conv_s)
    return (y_prompt, y_sample,
            new_k_w128_prompt, new_v_w128_prompt, new_k_w512_prompt, new_v_w512_prompt,
            new_k_w2048_prompt, new_v_w2048_prompt,
            new_k_w128_sample, new_v_w128_sample, new_k_w512_sample, new_v_w512_sample,
            new_k_w2048_sample, new_v_w2048_sample,
            new_state_pool_prompt, new_state_pool_sample,
            new_state_conv_prompt, new_state_conv_sample)
```

```python
import functools

import numpy as np
import jax
import jax.numpy as jnp
from jax import lax
from jax.experimental import pallas as pl
from jax.experimental.pallas import tpu as pltpu

D = 1024
HEAD_DIM = 64
N_HEADS = 16
N_PAIRS = N_HEADS // 2
WINDOWS = (128, 512, 2048)
DILATIONS = (1, 4, 16)
N_GROUPS = 3
SPAN = 128
N_BUCKETS = 32
MAX_DISTANCE = WINDOWS[-1]
POOL_WINDOWS = (2, 4, 8, 16)
POOL_GW = D // len(POOL_WINDOWS)
POOL_STATE = max(POOL_WINDOWS) - 1
D_FF = 11 * D // 4
CONV_WIDTH = 3
EPS = 1e-6
NEG = -1e30
SCALE = HEAD_DIM ** -0.5
LOG2E = 1.4426950408889634
F32 = jnp.float32
BF16 = jnp.bfloat16
HALO = 16
FF_CHUNK = 256
VMEM_LIMIT = 56 * 1024 * 1024


def _params(*sem):
    return pltpu.CompilerParams(dimension_semantics=sem, vmem_limit_bytes=VMEM_LIMIT)


def _rms(x, g):
    ms = jnp.mean(x * x, axis=-1, keepdims=True)
    return x * lax.rsqrt(ms + EPS) * g


def _silu(x):
    return x / (1.0 + jnp.exp(-x))


def _t5_buckets(dist):
    max_exact = N_BUCKETS // 2
    n = np.maximum(dist, 1).astype(np.float32)
    large = max_exact + (np.log(n / max_exact) / np.log(MAX_DISTANCE / max_exact)
                         * (N_BUCKETS - max_exact)).astype(np.int32)
    large = np.minimum(large, N_BUCKETS - 1)
    return np.where(dist < max_exact, dist, large).astype(np.int32)


def _group_bias(rel_bias, g):
    dist = np.arange(SPAN + 1) * DILATIONS[g]
    b = rel_bias[_t5_buckets(dist)]
    return b[:, g * N_HEADS:(g + 1) * N_HEADS].T.astype(F32)


def _prompt_bias_table(bg):
    qi = np.arange(SPAN)[None, :]
    kj = np.arange(2 * SPAN)[:, None]
    rel = qi - kj + SPAN
    valid = (rel >= 0) & (rel <= SPAN)
    tb = jnp.where(valid[None], LOG2E * bg[:, np.clip(rel, 0, SPAN)], NEG)
    tb = tb.reshape(N_PAIRS, 2, 2 * SPAN, SPAN).transpose(0, 2, 1, 3)
    return tb.reshape(N_PAIRS, 2 * SPAN, 2 * SPAN)


def _sample_bias_table(bg):
    back = SPAN - np.arange(SPAN)
    rows = jnp.concatenate([bg[:, back].T, bg[:, 0:1].T], axis=0)
    return jnp.pad(rows, ((0, 7), (0, 128 - N_HEADS)))


def _qkv_body(x_ref, g_ref, w_ref, qt_ref, k_ref, vt_ref, kt_ref, vtl_ref, *, tm):
    h = _rms(x_ref[...], g_ref[...]).astype(BF16)
    q = jnp.dot(h, w_ref[:, 0:D], preferred_element_type=F32)
    qt_ref[...] = (q * (SCALE * LOG2E)).T.astype(BF16)
    k = jnp.dot(h, w_ref[:, D:2 * D], preferred_element_type=F32)
    k_ref[...] = k.astype(BF16)
    kt_ref[...] = k[tm - SPAN:, :]
    v = jnp.dot(h, w_ref[:, 2 * D:3 * D], preferred_element_type=F32)
    vt_ref[...] = v.T.astype(BF16)
    vtl_ref[...] = v[tm - SPAN:, :]


def _qkv_stream(x, gain, w, g):
    B, T, _ = x.shape
    d = DILATIONS[g]
    ls = T // d
    tm = min(512, ls)
    xv = x.reshape(B, ls, d * D)
    body = functools.partial(_qkv_body, tm=tm)
    return pl.pallas_call(
        body,
        grid=(B, d, ls // tm),
        in_specs=[
            pl.BlockSpec((None, tm, D), lambda b, r, i: (b, i, r)),
            pl.BlockSpec((1, D), lambda b, r, i: (0, 0)),
            pl.BlockSpec((D, 3 * D), lambda b, r, i: (0, g)),
        ],
        out_specs=[
            pl.BlockSpec((None, None, D, tm), lambda b, r, i: (b, r, 0, i)),
            pl.BlockSpec((None, None, tm, D), lambda b, r, i: (b, r, i, 0)),
            pl.BlockSpec((None, None, D, tm), lambda b, r, i: (b, r, 0, i)),
            pl.BlockSpec((None, SPAN, D), lambda b, r, i: (b, 0, r)),
            pl.BlockSpec((None, SPAN, D), lambda b, r, i: (b, 0, r)),
        ],
        out_shape=[
            jax.ShapeDtypeStruct((B, d, D, ls), BF16),
            jax.ShapeDtypeStruct((B, d, ls, D), BF16),
            jax.ShapeDtypeStruct((B, d, D, ls), BF16),
            jax.ShapeDtypeStruct((B, SPAN, d * D), F32),
            jax.ShapeDtypeStruct((B, SPAN, d * D), F32),
        ],
        compiler_params=_params("arbitrary", "arbitrary", "arbitrary"),
        name=f"qkv_g{g}",
    )(xv, gain.reshape(1, D), w)


def _attn_body(*refs, first, last):
    qt_ref, kc_ref, kp_ref, vtc_ref, vtp_ref, bias_ref = refs[:6]
    refs = refs[6:]
    if not first:
        acc_in, m_in, l_in = refs[:3]
        refs = refs[3:]
    if last:
        o_ref = refs[0]
        refs = refs[1:]
    else:
        acc_out, m_out, l_out = refs[:3]
        refs = refs[3:]
    acct, m_sc, l_sc = refs

    pen = jnp.where(pl.program_id(2) == 0, NEG, 0.0).astype(F32)
    if not first:
        acct[...] = acc_in[...].T
        m_sc[...] = m_in[...].T[0:N_HEADS, :]
        l_sc[...] = l_in[...].T[0:N_HEADS, :]

    rows = lax.broadcasted_iota(jnp.int32, (2 * HEAD_DIM, SPAN), 0)
    lo = rows < HEAD_DIM
    mlo = jnp.where(lo, 1.0, 0.0).astype(BF16)
    mhi = jnp.where(lo, 0.0, 1.0).astype(BF16)

    def halves(row, shape=(2 * HEAD_DIM, SPAN)):
        return jnp.where(lo, jnp.broadcast_to(row[:, :SPAN], shape), jnp.broadcast_to(row[:, SPAN:], shape))

    for hp in range(N_PAIRS):
        r0 = hp * 2 * HEAD_DIM
        qt2 = qt_ref[r0:r0 + 2 * HEAD_DIM, :]
        qq = jnp.concatenate([qt2 * mlo, qt2 * mhi], axis=1)
        s_c = jnp.dot(kc_ref[:, r0:r0 + 2 * HEAD_DIM], qq, preferred_element_type=F32)
        s_c = s_c + bias_ref[hp, SPAN:2 * SPAN, :]
        s_p = jnp.dot(kp_ref[:, r0:r0 + 2 * HEAD_DIM], qq, preferred_element_type=F32)
        s_p = s_p + (bias_ref[hp, 0:SPAN, :] + pen)
        m_loc = jnp.max(jnp.maximum(s_c, s_p), axis=0, keepdims=True)
        if first:
            m_new = m_loc
        else:
            m_old = jnp.concatenate([m_sc[2 * hp:2 * hp + 1, :], m_sc[2 * hp + 1:2 * hp + 2, :]], axis=1)
            l_old = jnp.concatenate([l_sc[2 * hp:2 * hp + 1, :], l_sc[2 * hp + 1:2 * hp + 2, :]], axis=1)
            m_new = jnp.maximum(m_old, m_loc)
            alpha = jnp.exp2(m_old - m_new)
        p_c = jnp.exp2(s_c - m_new)
        p_p = jnp.exp2(s_p - m_new)
        l_new = jnp.sum(p_c + p_p, axis=0, keepdims=True)
        if not first:
            l_new = l_new + l_old * alpha
        vt2 = jnp.concatenate([vtp_ref[r0:r0 + 2 * HEAD_DIM, :], vtc_ref[r0:r0 + 2 * HEAD_DIM, :]], axis=1)
        pcat = jnp.concatenate([p_p, p_c], axis=0).astype(BF16)
        pv = jnp.dot(vt2, pcat, preferred_element_type=F32)
        new = jnp.where(lo, pv[:, :SPAN], pv[:, SPAN:])
        if not first:
            new = new + acct[r0:r0 + 2 * HEAD_DIM, :] * halves(alpha)
        if last:
            new = new * halves(1.0 / l_new)
        else:
            m_sc[2 * hp:2 * hp + 1, :] = m_new[:, :SPAN]
            m_sc[2 * hp + 1:2 * hp + 2, :] = m_new[:, SPAN:]
            l_sc[2 * hp:2 * hp + 1, :] = l_new[:, :SPAN]
            l_sc[2 * hp + 1:2 * hp + 2, :] = l_new[:, SPAN:]
        acct[r0:r0 + 2 * HEAD_DIM, :] = new

    if last:
        o_ref[...] = acct[...].T.astype(BF16)
    else:
        acc_out[...] = acct[...].T
        pad = jnp.zeros((SPAN - N_HEADS, SPAN), F32)
        m_out[...] = jnp.concatenate([m_sc[...], pad], axis=0).T
        l_out[...] = jnp.concatenate([l_sc[...], pad], axis=0).T


def _attn_pass(qt, k, vt, bias, state, g, B, T):
    d = DILATIONS[g]
    ls = T // d
    nb = ls // SPAN
    first = state is None
    last = g == N_GROUPS - 1
    cur = lambda b, r, s: (b, r, s, 0)
    prev = lambda b, r, s: (b, r, jnp.maximum(s - 1, 0), 0)
    cur_t = lambda b, r, s: (b, r, 0, s)
    prev_t = lambda b, r, s: (b, r, 0, jnp.maximum(s - 1, 0))
    nat = lambda b, r, s: (b, s, r)
    in_specs = [
        pl.BlockSpec((None, None, D, SPAN), cur_t),
        pl.BlockSpec((None, None, SPAN, D), cur),
        pl.BlockSpec((None, None, SPAN, D), prev),
        pl.BlockSpec((None, None, D, SPAN), cur_t),
        pl.BlockSpec((None, None, D, SPAN), prev_t),
        pl.BlockSpec((N_PAIRS, 2 * SPAN, 2 * SPAN), lambda b, r, s: (0, 0, 0)),
    ]
    args = [qt, k, k, vt, vt, bias]
    if not first:
        acc, m, l = state
        in_specs += [
            pl.BlockSpec((None, SPAN, D), nat),
            pl.BlockSpec((None, SPAN, 128), nat),
            pl.BlockSpec((None, SPAN, 128), nat),
        ]
        args += [acc.reshape(B, ls, d * D), m.reshape(B, ls, d * 128), l.reshape(B, ls, d * 128)]
    if last:
        out_specs = [pl.BlockSpec((None, SPAN, D), nat)]
        out_shape = [jax.ShapeDtypeStruct((B, ls, d * D), BF16)]
    else:
        out_specs = [
            pl.BlockSpec((None, SPAN, D), nat),
            pl.BlockSpec((None, SPAN, 128), nat),
            pl.BlockSpec((None, SPAN, 128), nat),
        ]
        out_shape = [
            jax.ShapeDtypeStruct((B, ls, d * D), F32),
            jax.ShapeDtypeStruct((B, ls, d * 128), F32),
            jax.ShapeDtypeStruct((B, ls, d * 128), F32),
        ]
    outs = pl.pallas_call(
        functools.partial(_attn_body, first=first, last=last),
        grid=(B, d, nb),
        in_specs=in_specs,
        out_specs=out_specs,
        out_shape=out_shape,
        scratch_shapes=[
            pltpu.VMEM((D, SPAN), F32),
            pltpu.VMEM((N_HEADS, SPAN), F32),
            pltpu.VMEM((N_HEADS, SPAN), F32),
        ],
        compiler_params=_params("arbitrary", "arbitrary", "arbitrary"),
        name=f"attn_g{g}",
    )(*args)
    if last:
        return outs[0].reshape(B, T, D)
    return (outs[0].reshape(B, T, D), outs[1].reshape(B, T, 128), outs[2].reshape(B, T, 128))


def _mm_res_body(a_ref, w_ref, x_ref, o_ref):
    o_ref[...] = x_ref[...] + jnp.dot(a_ref[...].astype(BF16), w_ref[...], preferred_element_type=F32)


def _mm_res(a, w, x):
    M, K = a.shape
    N = w.shape[1]
    tm = min(M, 1024)
    return pl.pallas_call(
        _mm_res_body,
        grid=(M // tm,),
        in_specs=[
            pl.BlockSpec((tm, K), lambda i: (i, 0)),
            pl.BlockSpec((K, N), lambda i: (0, 0)),
            pl.BlockSpec((tm, N), lambda i: (i, 0)),
        ],
        out_specs=pl.BlockSpec((tm, N), lambda i: (i, 0)),
        out_shape=jax.ShapeDtypeStruct((M, N), F32),
        compiler_params=_params("arbitrary"),
        name="mm_res",
    )(a, w, x)


def _norm_mm_body(x_ref, g_ref, w_ref, o_ref):
    h = _rms(x_ref[...], g_ref[...]).astype(BF16)
    o_ref[...] = jnp.dot(h, w_ref[...], preferred_element_type=F32)


def _norm_mm(x, gain, w, tn):
    M, K = x.shape
    N = w.shape[1]
    return pl.pallas_call(
        _norm_mm_body,
        grid=(N // tn,),
        in_specs=[
            pl.BlockSpec((M, K), lambda j: (0, 0)),
            pl.BlockSpec((1, K), lambda j: (0, 0)),
            pl.BlockSpec((K, tn), lambda j: (0, j)),
        ],
        out_specs=pl.BlockSpec((M, tn), lambda j: (0, j)),
        out_shape=jax.ShapeDtypeStruct((M, N), F32),
        compiler_params=_params("arbitrary"),
        name="norm_mm",
    )(x, gain.reshape(1, K), w)


def _ffn_body(*refs, tm, tpb, nj, final):
    x_ref, xh_ref, g_ref, wg_ref, wv_ref, cw_ref, cb_ref, wo_ref = refs[:8]
    refs = refs[8:]
    if final:
        gf_ref = refs[0]
        refs = refs[1:]
    y_ref, cs_ref, hbuf, ubuf, acc = refs
    i = pl.program_id(0)
    j = pl.program_id(1)

    @pl.when(j == 0)
    def _():
        g = g_ref[...]
        keep = jnp.where(i % tpb == 0, 0.0, 1.0).astype(F32)
        hbuf[0:HALO, :] = (_rms(xh_ref[...], g) * keep).astype(BF16)
        hbuf[HALO:, :] = _rms(x_ref[...], g).astype(BF16)

    ubuf[...] = jnp.dot(hbuf[...], wg_ref[...], preferred_element_type=F32)
    val = jnp.dot(hbuf[HALO:, :], wv_ref[...], preferred_element_type=F32)
    conv = cb_ref[...] + ubuf[pl.ds(HALO - 2, tm), :] * cw_ref[0:1, :]
    conv = conv + ubuf[pl.ds(HALO - 1, tm), :] * cw_ref[1:2, :]
    conv = conv + ubuf[pl.ds(HALO, tm), :] * cw_ref[2:3, :]
    act = (_silu(conv) * val).astype(BF16)
    part = jnp.dot(act, wo_ref[...], preferred_element_type=F32)
    cs_ref[...] = ubuf[pl.ds(tm + HALO - 8, 8), :]

    @pl.when(j == 0)
    def _():
        acc[...] = part

    @pl.when(j > 0)
    def _():
        acc[...] += part

    @pl.when(j == nj - 1)
    def _():
        y = x_ref[...] + acc[...]
        if final:
            y = _rms(y, gf_ref[...])
        y_ref[...] = y


def _ffn_prompt(x, T, gain, w_in, conv_w, conv_b, w_out, final_gain=None):
    P = x.shape[0]
    tm = 1024
    tpb = T // tm
    nj = D_FF // FF_CHUNK
    final = final_gain is not None
    in_specs = [
        pl.BlockSpec((tm, D), lambda i, j: (i, 0)),
        pl.BlockSpec((HALO, D), lambda i, j: (jnp.maximum(i * (tm // HALO) - 1, 0), 0)),
        pl.BlockSpec((1, D), lambda i, j: (0, 0)),
        pl.BlockSpec((D, FF_CHUNK), lambda i, j: (0, j)),
        pl.BlockSpec((D, FF_CHUNK), lambda i, j: (0, nj + j)),
        pl.BlockSpec((CONV_WIDTH, FF_CHUNK), lambda i, j: (0, j)),
        pl.BlockSpec((1, FF_CHUNK), lambda i, j: (0, j)),
        pl.BlockSpec((FF_CHUNK, D), lambda i, j: (j, 0)),
    ]
    args = [x, x, gain.reshape(1, D), w_in, w_in, conv_w, conv_b.reshape(1, D_FF), w_out]
    if final:
        in_specs.append(pl.BlockSpec((1, D), lambda i, j: (0, 0)))
        args.append(final_gain.reshape(1, D))
    return pl.pallas_call(
        functools.partial(_ffn_body, tm=tm, tpb=tpb, nj=nj, final=final),
        grid=(P // tm, nj),
        in_specs=in_specs,
        out_specs=[
            pl.BlockSpec((tm, D), lambda i, j: (i, 0)),
            pl.BlockSpec((None, 8, FF_CHUNK), lambda i, j: (i, 0, j)),
        ],
        out_shape=[
            jax.ShapeDtypeStruct((P, D), F32),
            jax.ShapeDtypeStruct((P // tm, 8, D_FF), F32),
        ],
        scratch_shapes=[
            pltpu.VMEM((tm + HALO, D), BF16),
            pltpu.VMEM((tm + HALO, FF_CHUNK), F32),
            pltpu.VMEM((tm, D), F32),
        ],
        compiler_params=_params("arbitrary", "arbitrary"),
        name="ffn_prompt",
    )(*args)


def _pool_body(x_ref, xh_ref, g_ref, wp_ref, sc_ref, y_ref, hs_ref, hbuf, *, tm, tpb):
    i = pl.program_id(0)
    g = g_ref[...]
    x = x_ref[...]
    h = _rms(x, g)
    keep = jnp.where(i % tpb == 0, 0.0, 1.0).astype(F32)
    hbuf[0:HALO, :] = _rms(xh_ref[...], g) * keep
    hbuf[HALO:, :] = h
    hs_ref[...] = h[tm - HALO:, :]
    pos = (i % tpb) * tm + lax.broadcasted_iota(jnp.int32, (tm, 1), 0)
    for gi, w in enumerate(POOL_WINDOWS):
        c0 = gi * POOL_GW
        ssum = hbuf[pl.ds(HALO, tm), c0:c0 + POOL_GW]
        for kk in range(1, w):
            ssum = ssum + hbuf[pl.ds(HALO - kk, tm), c0:c0 + POOL_GW]
        cnt = jnp.minimum(pos + 1, w).astype(F32)
        z = ssum / cnt - h[:, c0:c0 + POOL_GW]
        o = jnp.dot(z.astype(BF16), wp_ref[gi], preferred_element_type=F32)
        y_ref[:, c0:c0 + POOL_GW] = x[:, c0:c0 + POOL_GW] + o * sc_ref[:, c0:c0 + POOL_GW]


def _pool_prompt(x, T, gain, w_pool, scale):
    P = x.shape[0]
    tm = 512
    tpb = T // tm
    return pl.pallas_call(
        functools.partial(_pool_body, tm=tm, tpb=tpb),
        grid=(P // tm,),
        in_specs=[
            pl.BlockSpec((tm, D), lambda i: (i, 0)),
            pl.BlockSpec((HALO, D), lambda i: (jnp.maximum(i * (tm // HALO) - 1, 0), 0)),
            pl.BlockSpec((1, D), lambda i: (0, 0)),
            pl.BlockSpec((len(POOL_WINDOWS), POOL_GW, POOL_GW), lambda i: (0, 0, 0)),
            pl.BlockSpec((1, D), lambda i: (0, 0)),
        ],
        out_specs=[
            pl.BlockSpec((tm, D), lambda i: (i, 0)),
            pl.BlockSpec((None, HALO, D), lambda i: (i, 0, 0)),
        ],
        out_shape=[
            jax.ShapeDtypeStruct((P, D), F32),
            jax.ShapeDtypeStruct((P // tm, HALO, D), F32),
        ],
        scratch_shapes=[pltpu.VMEM((tm + HALO, D), F32)],
        compiler_params=_params("arbitrary"),
        name="pool_prompt",
    )(x, x, gain.reshape(1, D), w_pool, scale.reshape(1, D))


def _split_dot(x, w):
    hi = x.astype(BF16)
    lo = (x - hi.astype(F32)).astype(BF16)
    return jnp.dot(hi, w, preferred_element_type=F32) + jnp.dot(lo, w, preferred_element_type=F32)


def _sattn_body(qkv_ref, k0, v0, k1, v1, k2, v2, bs_ref, e_ref, et_ref, o_ref):
    e = e_ref[...]
    et = et_ref[...]

    def row8(r):
        return jnp.broadcast_to(r, (8, r.shape[1]))

    us, ms, ls = [], [], []
    for g, (kr, vr) in enumerate(((k0, v0), (k1, v1), (k2, v2))):
        c = g * 3 * D
        q = qkv_ref[:, c:c + D] * SCALE
        kn = qkv_ref[:, c + D:c + 2 * D]
        vn = qkv_ref[:, c + 2 * D:c + 3 * D]
        s = _split_dot(kr[...] * q, e) + bs_ref[g, 0:SPAN, :]
        s0 = _split_dot(row8(kn * q), e)[0:1, :] + bs_ref[g, SPAN:SPAN + 1, :]
        m = jnp.maximum(jnp.max(s, axis=0, keepdims=True), s0)
        p = jnp.exp(s - m)
        p0 = jnp.exp(s0 - m)
        ls.append(jnp.sum(p, axis=0, keepdims=True) + p0)
        u = jnp.sum(_split_dot(p, et) * vr[...], axis=0, keepdims=True)
        us.append(u + _split_dot(row8(p0), et)[0:1, :] * vn)
        ms.append(m)
    m_all = jnp.maximum(jnp.maximum(ms[0], ms[1]), ms[2])
    es = [jnp.exp(m - m_all) for m in ms]
    l_all = es[0] * ls[0] + es[1] * ls[1] + es[2] * ls[2]
    out = jnp.zeros((1, D), F32)
    for g in range(N_GROUPS):
        out = out + _split_dot(row8(es[g] / l_all), et)[0:1, :] * us[g]
    o_ref[...] = out


def _sample_attn(qkv, caches, a, bias_s, e, et):
    S = qkv.shape[0]
    in_specs = [pl.BlockSpec((None, 1, 3 * N_GROUPS * D), lambda b: (b, 0, 0))]
    args = [qkv.reshape(S, 1, 3 * N_GROUPS * D)]
    for g, (ck, cv) in enumerate(caches):
        d = DILATIONS[g]
        for c in (ck, cv):
            args.append(c.reshape(c.shape[0], S, SPAN, d * D))
            in_specs.append(pl.BlockSpec((None, None, SPAN, D), lambda b: (a, b, 0, 0)))
    in_specs += [
        pl.BlockSpec((N_GROUPS, SPAN + 8, 128), lambda b: (0, 0, 0)),
        pl.BlockSpec((D, 128), lambda b: (0, 0)),
        pl.BlockSpec((128, D), lambda b: (0, 0)),
    ]
    args += [bias_s, e, et]
    out = pl.pallas_call(
        _sattn_body,
        grid=(S,),
        in_specs=in_specs,
        out_specs=pl.BlockSpec((None, 1, D), lambda b: (b, 0, 0)),
        out_shape=jax.ShapeDtypeStruct((S, 1, D), F32),
        compiler_params=_params("arbitrary"),
        name="sample_attn",
    )(*args)
    return out.reshape(S, D)


def _spool_body(x_ref, st_ref, g_ref, wp_ref, sc_ref, y_ref, h_ref):
    x = x_ref[...]
    h = _rms(x, g_ref[...])
    h_ref[...] = h
    for gi, w in enumerate(POOL_WINDOWS):
        c0 = gi * POOL_GW
        ssum = h[:, c0:c0 + POOL_GW]
        for kk in range(1, w):
            ssum = ssum + st_ref[POOL_STATE - kk, :, c0:c0 + POOL_GW]
        z = ssum / float(w) - h[:, c0:c0 + POOL_GW]
        o = jnp.dot(z.astype(BF16), wp_ref[gi], preferred_element_type=F32)
        y_ref[:, c0:c0 + POOL_GW] = x[:, c0:c0 + POOL_GW] + o * sc_ref[:, c0:c0 + POOL_GW]


def _pool_sample(x, state_t, gain, w_pool, scale):
    S = x.shape[0]
    full = lambda shape: pl.BlockSpec(shape, lambda i: (0,) * len(shape))
    return pl.pallas_call(
        _spool_body,
        grid=(1,),
        in_specs=[full((S, D)), full((POOL_STATE, S, D)), full((1, D)),
                  full((len(POOL_WINDOWS), POOL_GW, POOL_GW)), full((1, D))],
        out_specs=[full((S, D)), full((S, D))],
        out_shape=[jax.ShapeDtypeStruct((S, D), F32), jax.ShapeDtypeStruct((S, D), F32)],
        compiler_params=_params("arbitrary"),
        name="pool_sample",
    )(x, state_t, gain.reshape(1, D), w_pool, scale.reshape(1, D))


def _sffn_body(*refs, final):
    u_ref, st_ref, cw_ref, cb_ref, wo_ref, x_ref = refs[:6]
    refs = refs[6:]
    if final:
        gf_ref = refs[0]
        refs = refs[1:]
    (y_ref,) = refs
    gate = u_ref[:, 0:D_FF]
    val = u_ref[:, D_FF:2 * D_FF]
    conv = cb_ref[...] + st_ref[0] * cw_ref[0:1, :]
    conv = conv + st_ref[1] * cw_ref[1:2, :]
    conv = conv + gate * cw_ref[2:3, :]
    act = (_silu(conv) * val).astype(BF16)
    y = x_ref[...] + jnp.dot(act, wo_ref[...], preferred_element_type=F32)
    if final:
        y = _rms(y, gf_ref[...])
    y_ref[...] = y


def _ffn_sample(u, state_t, conv_w, conv_b, w_out, x, final_gain=None):
    S = x.shape[0]
    final = final_gain is not None
    full = lambda shape: pl.BlockSpec(shape, lambda i: (0,) * len(shape))
    in_specs = [full((S, 2 * D_FF)), full((CONV_WIDTH - 1, S, D_FF)), full((CONV_WIDTH, D_FF)),
                full((1, D_FF)), full((D_FF, D)), full((S, D))]
    args = [u, state_t, conv_w, conv_b.reshape(1, D_FF), w_out, x]
    if final:
        in_specs.append(full((1, D)))
        args.append(final_gain.reshape(1, D))
    return pl.pallas_call(
        functools.partial(_sffn_body, final=final),
        grid=(1,),
        in_specs=in_specs,
        out_specs=full((S, D)),
        out_shape=jax.ShapeDtypeStruct((S, D), F32),
        compiler_params=_params("arbitrary"),
        name="ffn_sample",
    )(*args)


def kernel(x_prompt, x_sample, cache_k_w128, cache_v_w128, cache_k_w512, cache_v_w512, cache_k_w2048, cache_v_w2048, state_pool, state_conv, rel_bias, norm_mix, w_qkv, w_o, w_pool, pool_scale, norm_ffn, w_in, conv_w, conv_b, w_out, norm_final):
    B, T, _ = x_prompt.shape
    S = x_sample.shape[0]
    assert x_sample.shape[1] == 1 and T % (SPAN * DILATIONS[-1]) == 0
    depth = w_in.shape[0]
    P = B * T
    caches = ((cache_k_w128, cache_v_w128), (cache_k_w512, cache_v_w512), (cache_k_w2048, cache_v_w2048))
    for g, (ck, _) in enumerate(caches):
        assert ck.shape[2] == WINDOWS[g]

    wqkv = w_qkv.astype(BF16)
    wo = w_o.astype(BF16)
    wpool = w_pool.astype(BF16)
    win = w_in.astype(BF16)
    wout = w_out.astype(BF16)

    bgs = [_group_bias(rel_bias, g) for g in range(N_GROUPS)]
    bias_p = [_prompt_bias_table(bg) for bg in bgs]
    bias_s = jnp.stack([_sample_bias_table(bg) for bg in bgs])
    head_of_lane = np.arange(D) // HEAD_DIM
    e_np = (head_of_lane[:, None] == np.arange(128)[None, :]).astype(np.float32)
    e = jnp.asarray(e_np, BF16)
    et = jnp.asarray(e_np.T, BF16)

    xp = x_prompt.reshape(P, D)
    xs = x_sample.reshape(S, D)
    nk_p = [[] for _ in range(N_GROUPS)]
    nv_p = [[] for _ in range(N_GROUPS)]
    nk_s = [[] for _ in range(N_GROUPS)]
    nv_s = [[] for _ in range(N_GROUPS)]
    pool_p, pool_s, conv_p, conv_s = [], [], [], []

    for i in range(depth):
        if i % 2 == 0:
            a = i // 2
            state = None
            for g in range(N_GROUPS):
                d = DILATIONS[g]
                qt, k, vt, ktl, vtl = _qkv_stream(xp.reshape(B, T, D), norm_mix[i], wqkv[a], g)
                state = _attn_pass(qt, k, vt, bias_p[g], state, g, B, T)
                nk_p[g].append(ktl.reshape(B, SPAN * d, N_HEADS, HEAD_DIM))
                nv_p[g].append(vtl.reshape(B, SPAN * d, N_HEADS, HEAD_DIM))
            xp = _mm_res(state.reshape(P, D), wo[a], xp)

            qkv_s = _norm_mm(xs, norm_mix[i], wqkv[a], tn=1024)
            o_s = _sample_attn(qkv_s, caches, a, bias_s, e, et)
            xs = _mm_res(o_s, wo[a], xs)
            for g in range(N_GROUPS):
                c = g * 3 * D
                nk_s[g].append(qkv_s[:, c + D:c + 2 * D].reshape(S, 1, N_HEADS, HEAD_DIM))
                nv_s[g].append(qkv_s[:, c + 2 * D:c + 3 * D].reshape(S, 1, N_HEADS, HEAD_DIM))
        else:
            b = i // 2
            xp, hs = _pool_prompt(xp, T, norm_mix[i], wpool[b], pool_scale[b])
            tiles = hs.shape[0] // B
            pool_p.append(hs.reshape(B, tiles, HALO, D)[:, -1, HALO - POOL_STATE:, :])
            st = state_pool[b]
            xs, h_s = _pool_sample(xs, jnp.swapaxes(st, 0, 1), norm_mix[i], wpool[b], pool_scale[b])
            pool_s.append(jnp.concatenate([st[:, 1:], h_s[:, None, :]], axis=1))

        fg = norm_final if i == depth - 1 else None
        xp, cs = _ffn_prompt(xp, T, norm_ffn[i], win[i], conv_w[i], conv_b[i], wout[i], fg)
        tiles = cs.shape[0] // B
        conv_p.append(cs.reshape(B, tiles, 8, D_FF)[:, -1, 8 - (CONV_WIDTH - 1):, :])
        u_s = _norm_mm(xs, norm_ffn[i], win[i], tn=512)
        st = state_conv[i]
        xs = _ffn_sample(u_s, jnp.swapaxes(st, 0, 1), conv_w[i], conv_b[i], wout[i], xs, fg)
        conv_s.append(jnp.concatenate([st[:, 1:], u_s[:, None, :D_FF]], axis=1))

    y_prompt = xp.reshape(B, T, D)
    y_sample = xs.reshape(S, 1, D)
    outs = [y_prompt, y_sample]
    for g in range(N_GROUPS):
        outs += [jnp.stack(nk_p[g]), jnp.stack(nv_p[g])]
    for g in range(N_GROUPS):
        outs += [jnp.stack(nk_s[g]), jnp.stack(nv_s[g])]
    outs += [jnp.stack(pool_p), jnp.stack(pool_s), jnp.stack(conv_p), jnp.stack(conv_s)]
    return tuple(outs)
```

```python
import functools

import numpy as np
import jax
import jax.numpy as jnp
from jax import lax
from jax.experimental import pallas as pl
from jax.experimental.pallas import tpu as pltpu

D = 1024
HEAD_DIM = 64
N_HEADS = 16
N_PAIRS = N_HEADS // 2
WINDOWS = (128, 512, 2048)
DILATIONS = (1, 4, 16)
N_GROUPS = 3
SPAN = 128
N_BUCKETS = 32
MAX_DISTANCE = WINDOWS[-1]
POOL_WINDOWS = (2, 4, 8, 16)
POOL_GW = D // len(POOL_WINDOWS)
POOL_STATE = max(POOL_WINDOWS) - 1
D_FF = 11 * D // 4
CONV_WIDTH = 3
EPS = 1e-6
NEG = -1e30
SCALE = HEAD_DIM ** -0.5
LOG2E = 1.4426950408889634
F32 = jnp.float32
BF16 = jnp.bfloat16
HALO = 16
FF_CHUNK = 256
VMEM_LIMIT = 56 * 1024 * 1024


def _params(*sem):
    return pltpu.CompilerParams(dimension_semantics=sem, vmem_limit_bytes=VMEM_LIMIT)


def _rms(x, g):
    ms = jnp.mean(x * x, axis=-1, keepdims=True)
    return x * lax.rsqrt(ms + EPS) * g


def _silu(x):
    return x / (1.0 + jnp.exp(-x))


def _t5_buckets(dist):
    max_exact = N_BUCKETS // 2
    n = np.maximum(dist, 1).astype(np.float32)
    large = max_exact + (np.log(n / max_exact) / np.log(MAX_DISTANCE / max_exact)
                         * (N_BUCKETS - max_exact)).astype(np.int32)
    large = np.minimum(large, N_BUCKETS - 1)
    return np.where(dist < max_exact, dist, large).astype(np.int32)


def _group_bias(rel_bias, g):
    dist = np.arange(SPAN + 1) * DILATIONS[g]
    b = rel_bias[_t5_buckets(dist)]
    return b[:, g * N_HEADS:(g + 1) * N_HEADS].T.astype(F32)


def _prompt_bias_table(bg):
    nk = 2 * SPAN
    period = nk + SPAN
    w = jnp.full((N_HEADS, period), NEG, F32)
    w = w.at[:, SPAN - 1:nk].set(LOG2E * bg)
    reps = -(-(nk * (period + 1)) // period)
    a = jnp.tile(w, (1, reps))[:, :nk * (period + 1)].reshape(N_HEADS, nk, period + 1)
    tb = a[:, ::-1, :SPAN]
    tb = tb.reshape(N_PAIRS, 2, nk, SPAN).transpose(0, 2, 1, 3)
    return tb.reshape(N_PAIRS, nk, 2 * SPAN)


def _sample_bias_tables(bg, g):
    d = DILATIONS[g]
    back = SPAN - np.arange(SPAN)
    pos = jnp.full((N_HEADS, SPAN, d), NEG, F32).at[:, :, 0].set(bg[:, back])
    return pos.reshape(N_HEADS, SPAN * d), bg[:, 0]


def _qkv_body(x_ref, g_ref, w_ref, qt_ref, k_ref, vt_ref, kt_ref, vtl_ref, *, tm):
    h = _rms(x_ref[...], g_ref[...]).astype(BF16)
    q = jnp.dot(h, w_ref[:, 0:D], preferred_element_type=F32)
    qt_ref[...] = (q * (SCALE * LOG2E)).T.astype(BF16)
    k = jnp.dot(h, w_ref[:, D:2 * D], preferred_element_type=F32)
    k_ref[...] = k.astype(BF16)
    kt_ref[...] = k[tm - SPAN:, :]
    v = jnp.dot(h, w_ref[:, 2 * D:3 * D], preferred_element_type=F32)
    vt_ref[...] = v.T.astype(BF16)
    vtl_ref[...] = v[tm - SPAN:, :]


def _qkv_stream(x, gain, w, a, g):
    B, T, _ = x.shape
    d = DILATIONS[g]
    ls = T // d
    tm = min(512, ls)
    xv = x.reshape(B, ls, d * D)
    body = functools.partial(_qkv_body, tm=tm)
    return pl.pallas_call(
        body,
        grid=(B, d, ls // tm),
        in_specs=[
            pl.BlockSpec((None, tm, D), lambda b, r, i: (b, i, r)),
            pl.BlockSpec((1, D), lambda b, r, i: (0, 0)),
            pl.BlockSpec((None, D, 3 * D), lambda b, r, i: (a, 0, g)),
        ],
        out_specs=[
            pl.BlockSpec((None, None, D, tm), lambda b, r, i: (b, r, 0, i)),
            pl.BlockSpec((None, None, tm, D), lambda b, r, i: (b, r, i, 0)),
            pl.BlockSpec((None, None, D, tm), lambda b, r, i: (b, r, 0, i)),
            pl.BlockSpec((None, SPAN, D), lambda b, r, i: (b, 0, r)),
            pl.BlockSpec((None, SPAN, D), lambda b, r, i: (b, 0, r)),
        ],
        out_shape=[
            jax.ShapeDtypeStruct((B, d, D, ls), BF16),
            jax.ShapeDtypeStruct((B, d, ls, D), BF16),
            jax.ShapeDtypeStruct((B, d, D, ls), BF16),
            jax.ShapeDtypeStruct((B, SPAN, d * D), F32),
            jax.ShapeDtypeStruct((B, SPAN, d * D), F32),
        ],
        compiler_params=_params("arbitrary", "arbitrary", "arbitrary"),
        name=f"qkv_g{g}",
    )(xv, gain.reshape(1, D), w)


def _attn_body(*refs, first, last):
    qt_ref, kc_ref, kp_ref, vtc_ref, vtp_ref, bias_ref = refs[:6]
    refs = refs[6:]
    if not first:
        acc_in, m_in, l_in = refs[:3]
        refs = refs[3:]
    if last:
        o_ref = refs[0]
        refs = refs[1:]
    else:
        acc_out, m_out, l_out = refs[:3]
        refs = refs[3:]
    acct, m_sc, l_sc = refs

    pen = jnp.where(pl.program_id(2) == 0, NEG, 0.0).astype(F32)
    if not first:
        acct[...] = acc_in[...].T
        m_sc[...] = m_in[...].T[0:N_HEADS, :]
        l_sc[...] = l_in[...].T[0:N_HEADS, :]

    rows = lax.broadcasted_iota(jnp.int32, (2 * HEAD_DIM, SPAN), 0)
    lo = rows < HEAD_DIM
    mlo = jnp.where(lo, 1.0, 0.0).astype(BF16)
    mhi = jnp.where(lo, 0.0, 1.0).astype(BF16)

    def halves(row, shape=(2 * HEAD_DIM, SPAN)):
        return jnp.where(lo, jnp.broadcast_to(row[:, :SPAN], shape), jnp.broadcast_to(row[:, SPAN:], shape))

    for hp in range(N_PAIRS):
        r0 = hp * 2 * HEAD_DIM
        qt2 = qt_ref[r0:r0 + 2 * HEAD_DIM, :]
        qq = jnp.concatenate([qt2 * mlo, qt2 * mhi], axis=1)
        s_c = jnp.dot(kc_ref[:, r0:r0 + 2 * HEAD_DIM], qq, preferred_element_type=F32)
        s_c = s_c + bias_ref[hp, SPAN:2 * SPAN, :]
        s_p = jnp.dot(kp_ref[:, r0:r0 + 2 * HEAD_DIM], qq, preferred_element_type=F32)
        s_p = s_p + (bias_ref[hp, 0:SPAN, :] + pen)
        m_loc = jnp.max(jnp.maximum(s_c, s_p), axis=0, keepdims=True)
        if first:
            m_new = m_loc
        else:
            m_old = jnp.concatenate([m_sc[2 * hp:2 * hp + 1, :], m_sc[2 * hp + 1:2 * hp + 2, :]], axis=1)
            l_old = jnp.concatenate([l_sc[2 * hp:2 * hp + 1, :], l_sc[2 * hp + 1:2 * hp + 2, :]], axis=1)
            m_new = jnp.maximum(m_old, m_loc)
            alpha = jnp.exp2(m_old - m_new)
        p_c = jnp.exp2(s_c - m_new)
        p_p = jnp.exp2(s_p - m_new)
        l_new = jnp.sum(p_c + p_p, axis=0, keepdims=True)
        if not first:
            l_new = l_new + l_old * alpha
        vt2 = jnp.concatenate([vtp_ref[r0:r0 + 2 * HEAD_DIM, :], vtc_ref[r0:r0 + 2 * HEAD_DIM, :]], axis=1)
        pcat = jnp.concatenate([p_p, p_c], axis=0).astype(BF16)
        pv = jnp.dot(vt2, pcat, preferred_element_type=F32)
        new = jnp.where(lo, pv[:, :SPAN], pv[:, SPAN:])
        if not first:
            new = new + acct[r0:r0 + 2 * HEAD_DIM, :] * halves(alpha)
        if last:
            new = new * halves(1.0 / l_new)
        else:
            m_sc[2 * hp:2 * hp + 1, :] = m_new[:, :SPAN]
            m_sc[2 * hp + 1:2 * hp + 2, :] = m_new[:, SPAN:]
            l_sc[2 * hp:2 * hp + 1, :] = l_new[:, :SPAN]
            l_sc[2 * hp + 1:2 * hp + 2, :] = l_new[:, SPAN:]
        acct[r0:r0 + 2 * HEAD_DIM, :] = new

    if last:
        o_ref[...] = acct[...].T.astype(BF16)
    else:
        acc_out[...] = acct[...].T
        pad = jnp.zeros((SPAN - N_HEADS, SPAN), F32)
        m_out[...] = jnp.concatenate([m_sc[...], pad], axis=0).T
        l_out[...] = jnp.concatenate([l_sc[...], pad], axis=0).T


def _attn_pass(qt, k, vt, bias, state, g, B, T):
    d = DILATIONS[g]
    ls = T // d
    nb = ls // SPAN
    first = state is None
    last = g == N_GROUPS - 1
    cur = lambda b, r, s: (b, r, s, 0)
    prev = lambda b, r, s: (b, r, jnp.maximum(s - 1, 0), 0)
    cur_t = lambda b, r, s: (b, r, 0, s)
    prev_t = lambda b, r, s: (b, r, 0, jnp.maximum(s - 1, 0))
    nat = lambda b, r, s: (b, s, r)
    in_specs = [
        pl.BlockSpec((None, None, D, SPAN), cur_t),
        pl.BlockSpec((None, None, SPAN, D), cur),
        pl.BlockSpec((None, None, SPAN, D), prev),
        pl.BlockSpec((None, None, D, SPAN), cur_t),
        pl.BlockSpec((None, None, D, SPAN), prev_t),
        pl.BlockSpec((N_PAIRS, 2 * SPAN, 2 * SPAN), lambda b, r, s: (0, 0, 0)),
    ]
    args = [qt, k, k, vt, vt, bias]
    if not first:
        acc, m, l = state
        in_specs += [
            pl.BlockSpec((None, SPAN, D), nat),
            pl.BlockSpec((None, SPAN, 128), nat),
            pl.BlockSpec((None, SPAN, 128), nat),
        ]
        args += [acc.reshape(B, ls, d * D), m.reshape(B, ls, d * 128), l.reshape(B, ls, d * 128)]
    if last:
        out_specs = [pl.BlockSpec((None, SPAN, D), nat)]
        out_shape = [jax.ShapeDtypeStruct((B, ls, d * D), BF16)]
    else:
        out_specs = [
            pl.BlockSpec((None, SPAN, D), nat),
            pl.BlockSpec((None, SPAN, 128), nat),
            pl.BlockSpec((None, SPAN, 128), nat),
        ]
        out_shape = [
            jax.ShapeDtypeStruct((B, ls, d * D), F32),
            jax.ShapeDtypeStruct((B, ls, d * 128), F32),
            jax.ShapeDtypeStruct((B, ls, d * 128), F32),
        ]
    outs = pl.pallas_call(
        functools.partial(_attn_body, first=first, last=last),
        grid=(B, d, nb),
        in_specs=in_specs,
        out_specs=out_specs,
        out_shape=out_shape,
        scratch_shapes=[
            pltpu.VMEM((D, SPAN), F32),
            pltpu.VMEM((N_HEADS, SPAN), F32),
            pltpu.VMEM((N_HEADS, SPAN), F32),
        ],
        compiler_params=_params("arbitrary", "arbitrary", "arbitrary"),
        name=f"attn_g{g}",
    )(*args)
    if last:
        return outs[0].reshape(B, T, D)
    return (outs[0].reshape(B, T, D), outs[1].reshape(B, T, 128), outs[2].reshape(B, T, 128))


def _mm_res_body(a_ref, w_ref, x_ref, o_ref):
    o_ref[...] = x_ref[...] + jnp.dot(a_ref[...].astype(BF16), w_ref[...], preferred_element_type=F32)


def _mm_res(a, w, x):
    M, K = a.shape
    N = w.shape[1]
    tm = min(M, 1024)
    return pl.pallas_call(
        _mm_res_body,
        grid=(M // tm,),
        in_specs=[
            pl.BlockSpec((tm, K), lambda i: (i, 0)),
            pl.BlockSpec((K, N), lambda i: (0, 0)),
            pl.BlockSpec((tm, N), lambda i: (i, 0)),
        ],
        out_specs=pl.BlockSpec((tm, N), lambda i: (i, 0)),
        out_shape=jax.ShapeDtypeStruct((M, N), F32),
        compiler_params=_params("arbitrary"),
        name="mm_res",
    )(a, w, x)


def _norm_mm_body(x_ref, g_ref, w_ref, o_ref):
    h = _rms(x_ref[...], g_ref[...]).astype(BF16)
    o_ref[...] = jnp.dot(h, w_ref[...], preferred_element_type=F32)


def _norm_mm(x, gain, w, layer, tn):
    M, K = x.shape
    N = w.shape[2]
    return pl.pallas_call(
        _norm_mm_body,
        grid=(N // tn,),
        in_specs=[
            pl.BlockSpec((M, K), lambda j: (0, 0)),
            pl.BlockSpec((1, K), lambda j: (0, 0)),
            pl.BlockSpec((None, K, tn), lambda j: (layer, 0, j)),
        ],
        out_specs=pl.BlockSpec((M, tn), lambda j: (0, j)),
        out_shape=jax.ShapeDtypeStruct((M, N), F32),
        compiler_params=_params("arbitrary"),
        name="norm_mm",
    )(x, gain.reshape(1, K), w)


def _ffn_body(*refs, tm, tpb, final):
    x_ref, xh_ref, g_ref, win_ref, cw_ref, cb_ref, wo_ref = refs[:7]
    refs = refs[7:]
    if final:
        gf_ref = refs[0]
        refs = refs[1:]
    y_ref, cs_ref, hbuf, ubuf, abuf = refs
    i = pl.program_id(0)
    g = g_ref[...]
    keep = jnp.where(i % tpb == 0, 0.0, 1.0).astype(F32)
    hbuf[0:HALO, :] = (_rms(xh_ref[...], g) * keep).astype(BF16)
    hbuf[HALO:, :] = _rms(x_ref[...], g).astype(BF16)

    for j in range(D_FF // FF_CHUNK):
        c0 = j * FF_CHUNK
        ub = ubuf.at[j % 2]
        ub[...] = jnp.dot(hbuf[...], win_ref[:, c0:c0 + FF_CHUNK], preferred_element_type=F32)
        val = jnp.dot(hbuf[HALO:, :], win_ref[:, D_FF + c0:D_FF + c0 + FF_CHUNK], preferred_element_type=F32)
        conv = cb_ref[:, c0:c0 + FF_CHUNK] + ub[pl.ds(HALO - 2, tm), :] * cw_ref[0:1, c0:c0 + FF_CHUNK]
        conv = conv + ub[pl.ds(HALO - 1, tm), :] * cw_ref[1:2, c0:c0 + FF_CHUNK]
        conv = conv + ub[pl.ds(HALO, tm), :] * cw_ref[2:3, c0:c0 + FF_CHUNK]
        abuf[:, c0:c0 + FF_CHUNK] = (_silu(conv) * val).astype(BF16)
        cs_ref[:, c0:c0 + FF_CHUNK] = ub[pl.ds(tm + HALO - 8, 8), :]

    y = x_ref[...] + jnp.dot(abuf[...], wo_ref[...], preferred_element_type=F32)
    if final:
        y = _rms(y, gf_ref[...])
    y_ref[...] = y


def _ffn_prompt(x, T, gain, w_in, conv_w, conv_b, w_out, layer, final_gain=None):
    P = x.shape[0]
    tm = 512
    tpb = T // tm
    final = final_gain is not None
    once = pl.Buffered(1)
    in_specs = [
        pl.BlockSpec((tm, D), lambda i: (i, 0)),
        pl.BlockSpec((HALO, D), lambda i: (jnp.maximum(i * (tm // HALO) - 1, 0), 0)),
        pl.BlockSpec((1, D), lambda i: (0, 0)),
        pl.BlockSpec((None, D, 2 * D_FF), lambda i: (layer, 0, 0), pipeline_mode=once),
        pl.BlockSpec((CONV_WIDTH, D_FF), lambda i: (0, 0)),
        pl.BlockSpec((1, D_FF), lambda i: (0, 0)),
        pl.BlockSpec((None, D_FF, D), lambda i: (layer, 0, 0), pipeline_mode=once),
    ]
    args = [x, x, gain.reshape(1, D), w_in, conv_w, conv_b.reshape(1, D_FF), w_out]
    if final:
        in_specs.append(pl.BlockSpec((1, D), lambda i: (0, 0)))
        args.append(final_gain.reshape(1, D))
    return pl.pallas_call(
        functools.partial(_ffn_body, tm=tm, tpb=tpb, final=final),
        grid=(P // tm,),
        in_specs=in_specs,
        out_specs=[
            pl.BlockSpec((tm, D), lambda i: (i, 0)),
            pl.BlockSpec((None, 8, D_FF), lambda i: (i, 0, 0)),
        ],
        out_shape=[
            jax.ShapeDtypeStruct((P, D), F32),
            jax.ShapeDtypeStruct((P // tm, 8, D_FF), F32),
        ],
        scratch_shapes=[
            pltpu.VMEM((tm + HALO, D), BF16),
            pltpu.VMEM((2, tm + HALO, FF_CHUNK), F32),
            pltpu.VMEM((tm, D_FF), BF16),
        ],
        compiler_params=_params("arbitrary"),
        name="ffn_prompt",
    )(*args)


def _pool_body(x_ref, xh_ref, g_ref, wp_ref, sc_ref, y_ref, hs_ref, hbuf, *, tm, tpb):
    i = pl.program_id(0)
    g = g_ref[...]
    x = x_ref[...]
    h = _rms(x, g)
    keep = jnp.where(i % tpb == 0, 0.0, 1.0).astype(F32)
    hbuf[0:HALO, :] = _rms(xh_ref[...], g) * keep
    hbuf[HALO:, :] = h
    hs_ref[...] = h[tm - HALO:, :]
    pos = (i % tpb) * tm + lax.broadcasted_iota(jnp.int32, (tm, 1), 0)
    for gi, w in enumerate(POOL_WINDOWS):
        c0 = gi * POOL_GW
        ssum = hbuf[pl.ds(HALO, tm), c0:c0 + POOL_GW]
        for kk in range(1, w):
            ssum = ssum + hbuf[pl.ds(HALO - kk, tm), c0:c0 + POOL_GW]
        cnt = jnp.minimum(pos + 1, w).astype(F32)
        z = ssum / cnt - h[:, c0:c0 + POOL_GW]
        o = jnp.dot(z.astype(BF16), wp_ref[gi], preferred_element_type=F32)
        y_ref[:, c0:c0 + POOL_GW] = x[:, c0:c0 + POOL_GW] + o * sc_ref[:, c0:c0 + POOL_GW]


def _pool_prompt(x, T, gain, w_pool, scale):
    P = x.shape[0]
    tm = 512
    tpb = T // tm
    return pl.pallas_call(
        functools.partial(_pool_body, tm=tm, tpb=tpb),
        grid=(P // tm,),
        in_specs=[
            pl.BlockSpec((tm, D), lambda i: (i, 0)),
            pl.BlockSpec((HALO, D), lambda i: (jnp.maximum(i * (tm // HALO) - 1, 0), 0)),
            pl.BlockSpec((1, D), lambda i: (0, 0)),
            pl.BlockSpec((len(POOL_WINDOWS), POOL_GW, POOL_GW), lambda i: (0, 0, 0)),
            pl.BlockSpec((1, D), lambda i: (0, 0)),
        ],
        out_specs=[
            pl.BlockSpec((tm, D), lambda i: (i, 0)),
            pl.BlockSpec((None, HALO, D), lambda i: (i, 0, 0)),
        ],
        out_shape=[
            jax.ShapeDtypeStruct((P, D), F32),
            jax.ShapeDtypeStruct((P // tm, HALO, D), F32),
        ],
        scratch_shapes=[pltpu.VMEM((tm + HALO, D), F32)],
        compiler_params=_params("arbitrary"),
        name="pool_prompt",
    )(x, x, gain.reshape(1, D), w_pool, scale.reshape(1, D))


HEADS_PER_STEP = 8


def _sattn_body(qt_ref, knt_ref, vnt_ref, k0, v0, k1, v1, k2, v2, bp0, bp1, bp2, b0_ref, o_ref):
    lane = lax.broadcasted_iota(jnp.int32, (HEAD_DIM, 128), 1)
    out = jnp.zeros((HEAD_DIM, 128), F32)
    for hl in range(HEADS_PER_STEP):
        us, ms, ls = [], [], []
        for g, (kr, vr, bp) in enumerate(((k0, v0, bp0), (k1, v1, bp1), (k2, v2, bp2))):
            qc = qt_ref[g][:, hl:hl + 1] * SCALE
            bias = bp[hl:hl + 1, :]
            valid = bias > 0.5 * NEG
            s = jnp.sum(kr[hl] * qc, axis=0, keepdims=True)
            s = jnp.where(valid, s + bias, NEG)
            s0 = jnp.sum(knt_ref[g][:, hl:hl + 1] * qc, axis=0, keepdims=True) + b0_ref[g:g + 1, hl:hl + 1]
            m = jnp.maximum(jnp.max(s, axis=1, keepdims=True), s0)
            p = jnp.exp(s - m)
            p0 = jnp.exp(s0 - m)
            ls.append(jnp.sum(p, axis=1, keepdims=True) + p0)
            u = jnp.sum(jnp.where(valid, vr[hl] * p, 0.0), axis=1, keepdims=True)
            us.append(u + vnt_ref[g][:, hl:hl + 1] * p0)
            ms.append(m)
        m_all = jnp.maximum(jnp.maximum(ms[0], ms[1]), ms[2])
        es = [jnp.exp(m - m_all) for m in ms]
        l_all = es[0] * ls[0] + es[1] * ls[1] + es[2] * ls[2]
        oh = (es[0] / l_all) * us[0] + (es[1] / l_all) * us[1] + (es[2] / l_all) * us[2]
        out = jnp.where(lane == hl, oh, out)
    o_ref[...] = out


def _sample_attn(qkv, caches_t, a, bias_pos, bias_new):
    S = qkv.shape[0]
    nh = N_HEADS // HEADS_PER_STEP
    t = qkv.reshape(S, 3 * N_GROUPS, nh, HEADS_PER_STEP, HEAD_DIM).transpose(0, 1, 2, 4, 3)
    t = jnp.pad(t, ((0, 0),) * 4 + ((0, 128 - HEADS_PER_STEP),))
    t = t.reshape(S, N_GROUPS, 3, nh, HEAD_DIM, 128)
    small = pl.BlockSpec((None, N_GROUPS, None, HEAD_DIM, 128), lambda b, hh: (b, 0, hh, 0, 0))
    in_specs = [small, small, small]
    args = [t[:, :, 0], t[:, :, 1], t[:, :, 2]]
    for g, (ck, cv) in enumerate(caches_t):
        w = WINDOWS[g]
        for c in (ck, cv):
            args.append(c)
            in_specs.append(pl.BlockSpec((None, None, HEADS_PER_STEP, HEAD_DIM, w),
                                         lambda b, hh: (a, b, hh, 0, 0)))
    for g in range(N_GROUPS):
        args.append(bias_pos[g])
        in_specs.append(pl.BlockSpec((HEADS_PER_STEP, WINDOWS[g]), lambda b, hh: (hh, 0)))
    args.append(bias_new)
    in_specs.append(pl.BlockSpec((None, 8, 128), lambda b, hh: (hh, 0, 0)))
    out = pl.pallas_call(
        _sattn_body,
        grid=(S, nh),
        in_specs=in_specs,
        out_specs=pl.BlockSpec((None, None, HEAD_DIM, 128), lambda b, hh: (b, hh, 0, 0)),
        out_shape=jax.ShapeDtypeStruct((S, nh, HEAD_DIM, 128), F32),
        compiler_params=_params("arbitrary", "arbitrary"),
        name="sample_attn",
    )(*args)
    return out[..., :HEADS_PER_STEP].transpose(0, 1, 3, 2).reshape(S, D)


def _spool_body(x_ref, st_ref, g_ref, wp_ref, sc_ref, y_ref, h_ref):
    x = x_ref[...]
    h = _rms(x, g_ref[...])
    h_ref[...] = h
    for gi, w in enumerate(POOL_WINDOWS):
        c0 = gi * POOL_GW
        ssum = h[:, c0:c0 + POOL_GW]
        for kk in range(1, w):
            ssum = ssum + st_ref[POOL_STATE - kk, :, c0:c0 + POOL_GW]
        z = ssum / float(w) - h[:, c0:c0 + POOL_GW]
        o = jnp.dot(z.astype(BF16), wp_ref[gi], preferred_element_type=F32)
        y_ref[:, c0:c0 + POOL_GW] = x[:, c0:c0 + POOL_GW] + o * sc_ref[:, c0:c0 + POOL_GW]


def _pool_sample(x, state_t, gain, w_pool, scale):
    S = x.shape[0]
    full = lambda shape: pl.BlockSpec(shape, lambda i: (0,) * len(shape))
    return pl.pallas_call(
        _spool_body,
        grid=(1,),
        in_specs=[full((S, D)), full((POOL_STATE, S, D)), full((1, D)),
                  full((len(POOL_WINDOWS), POOL_GW, POOL_GW)), full((1, D))],
        out_specs=[full((S, D)), full((S, D))],
        out_shape=[jax.ShapeDtypeStruct((S, D), F32), jax.ShapeDtypeStruct((S, D), F32)],
        compiler_params=_params("arbitrary"),
        name="pool_sample",
    )(x, state_t, gain.reshape(1, D), w_pool, scale.reshape(1, D))


def _sffn_body(*refs, final):
    u_ref, st_ref, cw_ref, cb_ref, wo_ref, x_ref = refs[:6]
    refs = refs[6:]
    if final:
        gf_ref = refs[0]
        refs = refs[1:]
    (y_ref,) = refs
    gate = u_ref[:, 0:D_FF]
    val = u_ref[:, D_FF:2 * D_FF]
    conv = cb_ref[...] + st_ref[0] * cw_ref[0:1, :]
    conv = conv + st_ref[1] * cw_ref[1:2, :]
    conv = conv + gate * cw_ref[2:3, :]
    act = (_silu(conv) * val).astype(BF16)
    y = x_ref[...] + jnp.dot(act, wo_ref[...], preferred_element_type=F32)
    if final:
        y = _rms(y, gf_ref[...])
    y_ref[...] = y


def _ffn_sample(u, state_t, conv_w, conv_b, w_out, x, final_gain=None):
    S = x.shape[0]
    final = final_gain is not None
    full = lambda shape: pl.BlockSpec(shape, lambda i: (0,) * len(shape))
    in_specs = [full((S, 2 * D_FF)), full((CONV_WIDTH - 1, S, D_FF)), full((CONV_WIDTH, D_FF)),
                full((1, D_FF)), full((D_FF, D)), full((S, D))]
    args = [u, state_t, conv_w, conv_b.reshape(1, D_FF), w_out, x]
    if final:
        in_specs.append(full((1, D)))
        args.append(final_gain.reshape(1, D))
    return pl.pallas_call(
        functools.partial(_sffn_body, final=final),
        grid=(1,),
        in_specs=in_specs,
        out_specs=full((S, D)),
        out_shape=jax.ShapeDtypeStruct((S, D), F32),
        compiler_params=_params("arbitrary"),
        name="ffn_sample",
    )(*args)


def kernel(x_prompt, x_sample, cache_k_w128, cache_v_w128, cache_k_w512, cache_v_w512, cache_k_w2048, cache_v_w2048, state_pool, state_conv, rel_bias, norm_mix, w_qkv, w_o, w_pool, pool_scale, norm_ffn, w_in, conv_w, conv_b, w_out, norm_final):
    B, T, _ = x_prompt.shape
    S = x_sample.shape[0]
    assert x_sample.shape[1] == 1 and T % (SPAN * DILATIONS[-1]) == 0
    depth = w_in.shape[0]
    P = B * T
    caches = ((cache_k_w128, cache_v_w128), (cache_k_w512, cache_v_w512), (cache_k_w2048, cache_v_w2048))
    for g, (ck, _) in enumerate(caches):
        assert ck.shape[2] == WINDOWS[g]

    wqkv = w_qkv.astype(BF16)
    wo = w_o.astype(BF16)
    wpool = w_pool.astype(BF16)
    win = w_in.astype(BF16)
    wout = w_out.astype(BF16)

    bgs = [_group_bias(rel_bias, g) for g in range(N_GROUPS)]
    bias_p = [_prompt_bias_table(bg) for bg in bgs]
    sample_tabs = [_sample_bias_tables(bg, g) for g, bg in enumerate(bgs)]
    bias_pos = [t[0] for t in sample_tabs]
    nh = N_HEADS // HEADS_PER_STEP
    bias_new = jnp.stack([t[1] for t in sample_tabs]).reshape(N_GROUPS, nh, HEADS_PER_STEP)
    bias_new = jnp.pad(bias_new.transpose(1, 0, 2), ((0, 0), (0, 8 - N_GROUPS), (0, 128 - HEADS_PER_STEP)))
    caches_t = tuple((jnp.transpose(ck, (0, 1, 3, 4, 2)), jnp.transpose(cv, (0, 1, 3, 4, 2)))
                     for ck, cv in caches)

    xp = x_prompt.reshape(P, D)
    xs = x_sample.reshape(S, D)
    nk_p = [[] for _ in range(N_GROUPS)]
    nv_p = [[] for _ in range(N_GROUPS)]
    nk_s = [[] for _ in range(N_GROUPS)]
    nv_s = [[] for _ in range(N_GROUPS)]
    pool_p, pool_s, conv_p, conv_s = [], [], [], []

    for i in range(depth):
        if i % 2 == 0:
            a = i // 2
            state = None
            for g in range(N_GROUPS):
                d = DILATIONS[g]
                qt, k, vt, ktl, vtl = _qkv_stream(xp.reshape(B, T, D), norm_mix[i], wqkv, a, g)
                state = _attn_pass(qt, k, vt, bias_p[g], state, g, B, T)
                nk_p[g].append(ktl.reshape(B, SPAN * d, N_HEADS, HEAD_DIM))
                nv_p[g].append(vtl.reshape(B, SPAN * d, N_HEADS, HEAD_DIM))
            xp = _mm_res(state.reshape(P, D), wo[a], xp)

            qkv_s = _norm_mm(xs, norm_mix[i], wqkv, a, tn=1024)
            o_s = _sample_attn(qkv_s, caches_t, a, bias_pos, bias_new)
            xs = _mm_res(o_s, wo[a], xs)
            for g in range(N_GROUPS):
                c = g * 3 * D
                nk_s[g].append(qkv_s[:, c + D:c + 2 * D].reshape(S, 1, N_HEADS, HEAD_DIM))
                nv_s[g].append(qkv_s[:, c + 2 * D:c + 3 * D].reshape(S, 1, N_HEADS, HEAD_DIM))
        else:
            b = i // 2
            xp, hs = _pool_prompt(xp, T, norm_mix[i], wpool[b], pool_scale[b])
            tiles = hs.shape[0] // B
            pool_p.append(hs.reshape(B, tiles, HALO, D)[:, -1, HALO - POOL_STATE:, :])
            st = state_pool[b]
            xs, h_s = _pool_sample(xs, jnp.swapaxes(st, 0, 1), norm_mix[i], wpool[b], pool_scale[b])
            pool_s.append(jnp.concatenate([st[:, 1:], h_s[:, None, :]], axis=1))

        fg = norm_final if i == depth - 1 else None
        xp, cs = _ffn_prompt(xp, T, norm_ffn[i], win, conv_w[i], conv_b[i], wout, i, fg)
        tiles = cs.shape[0] // B
        conv_p.append(cs.reshape(B, tiles, 8, D_FF)[:, -1, 8 - (CONV_WIDTH - 1):, :])
        u_s = _norm_mm(xs, norm_ffn[i], win, i, tn=512)
        st = state_conv[i]
        xs = _ffn_sample(u_s, jnp.swapaxes(st, 0, 1), conv_w[i], conv_b[i], wout[i], xs, fg)
        conv_s.append(jnp.concatenate([st[:, 1:], u_s[:, None, :D_FF]], axis=1))

    y_prompt = xp.reshape(B, T, D)
    y_sample = xs.reshape(S, 1, D)
    outs = [y_prompt, y_sample]
    for g in range(N_GROUPS):
        outs += [jnp.stack(nk_p[g]), jnp.stack(nv_p[g])]
    for g in range(N_GROUPS):
        outs += [jnp.stack(nk_s[g]), jnp.stack(nv_s[g])]
    outs += [jnp.stack(pool_p), jnp.stack(pool_s), jnp.stack(conv_p), jnp.stack(conv_s)]
    return tuple(outs)
```

```python
import functools

import numpy as np
import jax
import jax.numpy as jnp
from jax import lax
from jax.experimental import pallas as pl
from jax.experimental.pallas import tpu as pltpu

D = 1024
HEAD_DIM = 64
N_HEADS = 16
N_PAIRS = N_HEADS // 2
WINDOWS = (128, 512, 2048)
DILATIONS = (1, 4, 16)
N_GROUPS = 3
SPAN = 128
N_BUCKETS = 32
MAX_DISTANCE = WINDOWS[-1]
POOL_WINDOWS = (2, 4, 8, 16)
POOL_GW = D // len(POOL_WINDOWS)
POOL_STATE = max(POOL_WINDOWS) - 1
D_FF = 11 * D // 4
CONV_WIDTH = 3
EPS = 1e-6
NEG = -1e30
SCALE = HEAD_DIM ** -0.5
LOG2E = 1.4426950408889634
F32 = jnp.float32
BF16 = jnp.bfloat16
HALO = 16
FF_CHUNK = 256
VMEM_LIMIT = 56 * 1024 * 1024


def _params(*sem, flags=None):
    return pltpu.CompilerParams(dimension_semantics=sem, vmem_limit_bytes=VMEM_LIMIT, flags=flags)


def _rms(x, g):
    ms = jnp.mean(x * x, axis=-1, keepdims=True)
    return x * lax.rsqrt(ms + EPS) * g


def _silu(x):
    return x / (1.0 + jnp.exp(-x))


def _t5_buckets(dist):
    max_exact = N_BUCKETS // 2
    n = np.maximum(dist, 1).astype(np.float32)
    large = max_exact + (np.log(n / max_exact) / np.log(MAX_DISTANCE / max_exact)
                         * (N_BUCKETS - max_exact)).astype(np.int32)
    large = np.minimum(large, N_BUCKETS - 1)
    return np.where(dist < max_exact, dist, large).astype(np.int32)


def _group_bias(rel_bias, g):
    dist = np.arange(SPAN + 1) * DILATIONS[g]
    b = rel_bias[_t5_buckets(dist)]
    return b[:, g * N_HEADS:(g + 1) * N_HEADS].T.astype(F32)


def _prompt_bias_table(bg):
    nk = 2 * SPAN
    period = nk + SPAN
    w = jnp.full((N_HEADS, period), NEG, F32)
    w = w.at[:, SPAN - 1:nk].set(LOG2E * bg)
    reps = -(-(nk * (period + 1)) // period)
    a = jnp.tile(w, (1, reps))[:, :nk * (period + 1)].reshape(N_HEADS, nk, period + 1)
    tb = a[:, ::-1, :SPAN]
    tb = tb.reshape(N_PAIRS, 2, nk, SPAN).transpose(0, 2, 1, 3)
    return tb.reshape(N_PAIRS, nk, 2 * SPAN)


def _sample_bias_tables(bg, g):
    d = DILATIONS[g]
    back = SPAN - np.arange(SPAN)
    pos = jnp.full((N_HEADS, SPAN, d), NEG, F32).at[:, :, 0].set(bg[:, back])
    return pos.reshape(N_HEADS, SPAN * d), bg[:, 0]


def _qkv_body(x_ref, g_ref, w_ref, qt_ref, k_ref, vt_ref, kt_ref, vtl_ref, *, tm):
    h = _rms(x_ref[...], g_ref[...]).astype(BF16)
    q = jnp.dot(h, w_ref[:, 0:D], preferred_element_type=F32)
    qt_ref[...] = (q * (SCALE * LOG2E)).T.astype(BF16)
    k = jnp.dot(h, w_ref[:, D:2 * D], preferred_element_type=F32)
    k_ref[...] = k.astype(BF16)
    kt_ref[...] = k[tm - SPAN:, :]
    v = jnp.dot(h, w_ref[:, 2 * D:3 * D], preferred_element_type=F32)
    vt_ref[...] = v.T.astype(BF16)
    vtl_ref[...] = v[tm - SPAN:, :]


def _qkv_stream(x, gain, w, a, g):
    B, T, _ = x.shape
    d = DILATIONS[g]
    ls = T // d
    tm = min(512, ls)
    xv = x.reshape(B, ls, d * D)
    body = functools.partial(_qkv_body, tm=tm)
    return pl.pallas_call(
        body,
        grid=(B, d, ls // tm),
        in_specs=[
            pl.BlockSpec((None, tm, D), lambda b, r, i: (b, i, r)),
            pl.BlockSpec((1, D), lambda b, r, i: (0, 0)),
            pl.BlockSpec((None, D, 3 * D), lambda b, r, i: (a, 0, g)),
        ],
        out_specs=[
            pl.BlockSpec((None, None, D, tm), lambda b, r, i: (b, r, 0, i)),
            pl.BlockSpec((None, None, tm, D), lambda b, r, i: (b, r, i, 0)),
            pl.BlockSpec((None, None, D, tm), lambda b, r, i: (b, r, 0, i)),
            pl.BlockSpec((None, SPAN, D), lambda b, r, i: (b, 0, r)),
            pl.BlockSpec((None, SPAN, D), lambda b, r, i: (b, 0, r)),
        ],
        out_shape=[
            jax.ShapeDtypeStruct((B, d, D, ls), BF16),
            jax.ShapeDtypeStruct((B, d, ls, D), BF16),
            jax.ShapeDtypeStruct((B, d, D, ls), BF16),
            jax.ShapeDtypeStruct((B, SPAN, d * D), F32),
            jax.ShapeDtypeStruct((B, SPAN, d * D), F32),
        ],
        compiler_params=_params("arbitrary", "arbitrary", "arbitrary"),
        name=f"qkv_g{g}",
    )(xv, gain.reshape(1, D), w)


def _attn_body(*refs, first, last):
    qt_ref, kc_ref, kp_ref, vtc_ref, vtp_ref, bias_ref = refs[:6]
    refs = refs[6:]
    if not first:
        acc_in, m_in, l_in = refs[:3]
        refs = refs[3:]
    if last:
        o_ref = refs[0]
        refs = refs[1:]
    else:
        acc_out, m_out, l_out = refs[:3]
        refs = refs[3:]
    s_scr, pv_scr = refs

    pen = jnp.where(pl.program_id(2) == 0, NEG, 0.0).astype(F32)
    if not first:
        m_all = m_in[...].T[0:N_HEADS, :]
        l_all = l_in[...].T[0:N_HEADS, :]
    m_rows, l_rows = [], []

    rows = lax.broadcasted_iota(jnp.int32, (2 * HEAD_DIM, SPAN), 0)
    lo = rows < HEAD_DIM
    mlo = jnp.where(lo, 1.0, 0.0).astype(BF16)
    mhi = jnp.where(lo, 0.0, 1.0).astype(BF16)

    def halves(row, shape=(2 * HEAD_DIM, SPAN)):
        return jnp.where(lo, jnp.broadcast_to(row[:, :SPAN], shape), jnp.broadcast_to(row[:, SPAN:], shape))

    def stage_scores(hp):
        r0 = hp * 2 * HEAD_DIM
        qt2 = qt_ref[r0:r0 + 2 * HEAD_DIM, :]
        qq = jnp.concatenate([qt2 * mlo, qt2 * mhi], axis=1)
        s_scr[hp % 3, 0] = jnp.dot(kc_ref[:, r0:r0 + 2 * HEAD_DIM], qq, preferred_element_type=F32)
        s_scr[hp % 3, 1] = jnp.dot(kp_ref[:, r0:r0 + 2 * HEAD_DIM], qq, preferred_element_type=F32)

    def stage_softmax(hp):
        r0 = hp * 2 * HEAD_DIM
        s_c = s_scr[hp % 3, 0] + bias_ref[hp, SPAN:2 * SPAN, :]
        s_p = s_scr[hp % 3, 1] + (bias_ref[hp, 0:SPAN, :] + pen)
        m_loc = jnp.max(jnp.maximum(s_c, s_p), axis=0, keepdims=True)
        alpha = None
        if first:
            m_new = m_loc
        else:
            m_old = jnp.concatenate([m_all[2 * hp:2 * hp + 1, :], m_all[2 * hp + 1:2 * hp + 2, :]], axis=1)
            l_old = jnp.concatenate([l_all[2 * hp:2 * hp + 1, :], l_all[2 * hp + 1:2 * hp + 2, :]], axis=1)
            m_new = jnp.maximum(m_old, m_loc)
            alpha = jnp.exp2(m_old - m_new)
        p_c = jnp.exp2(s_c - m_new)
        p_p = jnp.exp2(s_p - m_new)
        l_new = jnp.sum(p_c + p_p, axis=0, keepdims=True)
        if not first:
            l_new = l_new + l_old * alpha
        vt2 = jnp.concatenate([vtp_ref[r0:r0 + 2 * HEAD_DIM, :], vtc_ref[r0:r0 + 2 * HEAD_DIM, :]], axis=1)
        pcat = jnp.concatenate([p_p, p_c], axis=0).astype(BF16)
        pv_scr[hp % 2] = jnp.dot(vt2, pcat, preferred_element_type=F32)
        return m_new, l_new, alpha

    def stage_update(hp, m_new, l_new, alpha):
        r0 = hp * 2 * HEAD_DIM
        new = jnp.where(lo, pv_scr[hp % 2, :, :SPAN], pv_scr[hp % 2, :, SPAN:])
        if not first:
            new = new + acc_in[:, r0:r0 + 2 * HEAD_DIM].T * halves(alpha)
        if last:
            new = new * halves(1.0 / l_new)
            o_ref[:, r0:r0 + 2 * HEAD_DIM] = new.T.astype(BF16)
        else:
            m_rows.extend([m_new[:, :SPAN], m_new[:, SPAN:]])
            l_rows.extend([l_new[:, :SPAN], l_new[:, SPAN:]])
            acc_out[:, r0:r0 + 2 * HEAD_DIM] = new.T

    stage_scores(0)
    stage_scores(1)
    pending = None
    for hp in range(N_PAIRS):
        if hp + 2 < N_PAIRS:
            stage_scores(hp + 2)
        stats = stage_softmax(hp)
        if pending is not None:
            stage_update(*pending)
        pending = (hp,) + stats
    stage_update(*pending)

    if not last:
        pad = jnp.zeros((SPAN - N_HEADS, SPAN), F32)
        m_out[...] = jnp.concatenate(m_rows + [pad], axis=0).T
        l_out[...] = jnp.concatenate(l_rows + [pad], axis=0).T


def _attn_pass(qt, k, vt, bias, state, g, B, T):
    d = DILATIONS[g]
    ls = T // d
    nb = ls // SPAN
    first = state is None
    last = g == N_GROUPS - 1
    cur = lambda b, r, s: (b, r, s, 0)
    prev = lambda b, r, s: (b, r, jnp.maximum(s - 1, 0), 0)
    cur_t = lambda b, r, s: (b, r, 0, s)
    prev_t = lambda b, r, s: (b, r, 0, jnp.maximum(s - 1, 0))
    nat = lambda b, r, s: (b, s, r)
    in_specs = [
        pl.BlockSpec((None, None, D, SPAN), cur_t),
        pl.BlockSpec((None, None, SPAN, D), cur),
        pl.BlockSpec((None, None, SPAN, D), prev),
        pl.BlockSpec((None, None, D, SPAN), cur_t),
        pl.BlockSpec((None, None, D, SPAN), prev_t),
        pl.BlockSpec((N_PAIRS, 2 * SPAN, 2 * SPAN), lambda b, r, s: (0, 0, 0)),
    ]
    args = [qt, k, k, vt, vt, bias]
    if not first:
        acc, m, l = state
        in_specs += [
            pl.BlockSpec((None, SPAN, D), nat),
            pl.BlockSpec((None, SPAN, 128), nat),
            pl.BlockSpec((None, SPAN, 128), nat),
        ]
        args += [acc.reshape(B, ls, d * D), m.reshape(B, ls, d * 128), l.reshape(B, ls, d * 128)]
    if last:
        out_specs = [pl.BlockSpec((None, SPAN, D), nat)]
        out_shape = [jax.ShapeDtypeStruct((B, ls, d * D), BF16)]
    else:
        out_specs = [
            pl.BlockSpec((None, SPAN, D), nat),
            pl.BlockSpec((None, SPAN, 128), nat),
            pl.BlockSpec((None, SPAN, 128), nat),
        ]
        out_shape = [
            jax.ShapeDtypeStruct((B, ls, d * D), F32),
            jax.ShapeDtypeStruct((B, ls, d * 128), F32),
            jax.ShapeDtypeStruct((B, ls, d * 128), F32),
        ]
    outs = pl.pallas_call(
        functools.partial(_attn_body, first=first, last=last),
        grid=(B, d, nb),
        in_specs=in_specs,
        out_specs=out_specs,
        out_shape=out_shape,
        scratch_shapes=[pltpu.VMEM((3, 2, SPAN, 2 * SPAN), F32),
                        pltpu.VMEM((2, 2 * HEAD_DIM, 2 * SPAN), F32)],
        compiler_params=_params("arbitrary", "arbitrary", "arbitrary"),
        name=f"attn_g{g}",
    )(*args)
    if last:
        return outs[0].reshape(B, T, D)
    return (outs[0].reshape(B, T, D), outs[1].reshape(B, T, 128), outs[2].reshape(B, T, 128))


def _mm_res_body(a_ref, w_ref, x_ref, o_ref):
    o_ref[...] = x_ref[...] + jnp.dot(a_ref[...].astype(BF16), w_ref[...], preferred_element_type=F32)


def _mm_res(a, w, x):
    M, K = a.shape
    N = w.shape[1]
    tm = min(M, 1024)
    return pl.pallas_call(
        _mm_res_body,
        grid=(M // tm,),
        in_specs=[
            pl.BlockSpec((tm, K), lambda i: (i, 0)),
            pl.BlockSpec((K, N), lambda i: (0, 0)),
            pl.BlockSpec((tm, N), lambda i: (i, 0)),
        ],
        out_specs=pl.BlockSpec((tm, N), lambda i: (i, 0)),
        out_shape=jax.ShapeDtypeStruct((M, N), F32),
        compiler_params=_params("arbitrary"),
        name="mm_res",
    )(a, w, x)


def _norm_mm_body(x_ref, g_ref, w_ref, o_ref):
    h = _rms(x_ref[...], g_ref[...]).astype(BF16)
    o_ref[...] = jnp.dot(h, w_ref[...], preferred_element_type=F32)


def _norm_mm(x, gain, w, layer, tn):
    M, K = x.shape
    N = w.shape[2]
    return pl.pallas_call(
        _norm_mm_body,
        grid=(N // tn,),
        in_specs=[
            pl.BlockSpec((M, K), lambda j: (0, 0)),
            pl.BlockSpec((1, K), lambda j: (0, 0)),
            pl.BlockSpec((None, K, tn), lambda j: (layer, 0, j)),
        ],
        out_specs=pl.BlockSpec((M, tn), lambda j: (0, j)),
        out_shape=jax.ShapeDtypeStruct((M, N), F32),
        compiler_params=_params("arbitrary"),
        name="norm_mm",
    )(x, gain.reshape(1, K), w)


def _ffn_body(*refs, tm, tpb, final):
    x_ref, xh_ref, g_ref, win_ref, cw_ref, cb_ref, wo_ref = refs[:7]
    refs = refs[7:]
    if final:
        gf_ref = refs[0]
        refs = refs[1:]
    y_ref, cs_ref, hbuf, ubuf, abuf = refs
    i = pl.program_id(0)
    g = g_ref[...]
    keep = jnp.where(i % tpb == 0, 0.0, 1.0).astype(F32)
    hbuf[0:HALO, :] = (_rms(xh_ref[...], g) * keep).astype(BF16)
    hbuf[HALO:, :] = _rms(x_ref[...], g).astype(BF16)

    for j in range(D_FF // FF_CHUNK):
        c0 = j * FF_CHUNK
        ub = ubuf.at[j % 2]
        ub[...] = jnp.dot(hbuf[...], win_ref[:, c0:c0 + FF_CHUNK], preferred_element_type=F32)
        val = jnp.dot(hbuf[HALO:, :], win_ref[:, D_FF + c0:D_FF + c0 + FF_CHUNK], preferred_element_type=F32)
        conv = cb_ref[:, c0:c0 + FF_CHUNK] + ub[pl.ds(HALO - 2, tm), :] * cw_ref[0:1, c0:c0 + FF_CHUNK]
        conv = conv + ub[pl.ds(HALO - 1, tm), :] * cw_ref[1:2, c0:c0 + FF_CHUNK]
        conv = conv + ub[pl.ds(HALO, tm), :] * cw_ref[2:3, c0:c0 + FF_CHUNK]
        abuf[:, c0:c0 + FF_CHUNK] = (_silu(conv) * val).astype(BF16)
        cs_ref[:, c0:c0 + FF_CHUNK] = ub[pl.ds(tm + HALO - 8, 8), :]

    y = x_ref[...] + jnp.dot(abuf[...], wo_ref[...], preferred_element_type=F32)
    if final:
        y = _rms(y, gf_ref[...])
    y_ref[...] = y


def _ffn_prompt(x, T, gain, w_in, conv_w, conv_b, w_out, layer, final_gain=None):
    P = x.shape[0]
    tm = 512
    tpb = T // tm
    final = final_gain is not None
    once = pl.Buffered(1)
    in_specs = [
        pl.BlockSpec((tm, D), lambda i: (i, 0)),
        pl.BlockSpec((HALO, D), lambda i: (jnp.maximum(i * (tm // HALO) - 1, 0), 0)),
        pl.BlockSpec((1, D), lambda i: (0, 0)),
        pl.BlockSpec((None, D, 2 * D_FF), lambda i: (layer, 0, 0), pipeline_mode=once),
        pl.BlockSpec((CONV_WIDTH, D_FF), lambda i: (0, 0)),
        pl.BlockSpec((1, D_FF), lambda i: (0, 0)),
        pl.BlockSpec((None, D_FF, D), lambda i: (layer, 0, 0), pipeline_mode=once),
    ]
    args = [x, x, gain.reshape(1, D), w_in, conv_w, conv_b.reshape(1, D_FF), w_out]
    if final:
        in_specs.append(pl.BlockSpec((1, D), lambda i: (0, 0)))
        args.append(final_gain.reshape(1, D))
    return pl.pallas_call(
        functools.partial(_ffn_body, tm=tm, tpb=tpb, final=final),
        grid=(P // tm,),
        in_specs=in_specs,
        out_specs=[
            pl.BlockSpec((tm, D), lambda i: (i, 0)),
            pl.BlockSpec((None, 8, D_FF), lambda i: (i, 0, 0)),
        ],
        out_shape=[
            jax.ShapeDtypeStruct((P, D), F32),
            jax.ShapeDtypeStruct((P // tm, 8, D_FF), F32),
        ],
        scratch_shapes=[
            pltpu.VMEM((tm + HALO, D), BF16),
            pltpu.VMEM((2, tm + HALO, FF_CHUNK), F32),
            pltpu.VMEM((tm, D_FF), BF16),
        ],
        compiler_params=_params("arbitrary"),
        name="ffn_prompt",
    )(*args)


def _pool_body(x_ref, xh_ref, g_ref, wp_ref, sc_ref, y_ref, hs_ref, hbuf, *, tm, tpb):
    i = pl.program_id(0)
    g = g_ref[...]
    x = x_ref[...]
    h = _rms(x, g)
    keep = jnp.where(i % tpb == 0, 0.0, 1.0).astype(F32)
    hbuf[0:HALO, :] = _rms(xh_ref[...], g) * keep
    hbuf[HALO:, :] = h
    hs_ref[...] = h[tm - HALO:, :]
    pos = (i % tpb) * tm + lax.broadcasted_iota(jnp.int32, (tm, 1), 0)
    for gi, w in enumerate(POOL_WINDOWS):
        c0 = gi * POOL_GW
        ssum = hbuf[pl.ds(HALO, tm), c0:c0 + POOL_GW]
        for kk in range(1, w):
            ssum = ssum + hbuf[pl.ds(HALO - kk, tm), c0:c0 + POOL_GW]
        cnt = jnp.minimum(pos + 1, w).astype(F32)
        z = ssum / cnt - h[:, c0:c0 + POOL_GW]
        o = jnp.dot(z.astype(BF16), wp_ref[gi], preferred_element_type=F32)
        y_ref[:, c0:c0 + POOL_GW] = x[:, c0:c0 + POOL_GW] + o * sc_ref[:, c0:c0 + POOL_GW]


def _pool_prompt(x, T, gain, w_pool, scale):
    P = x.shape[0]
    tm = 512
    tpb = T // tm
    return pl.pallas_call(
        functools.partial(_pool_body, tm=tm, tpb=tpb),
        grid=(P // tm,),
        in_specs=[
            pl.BlockSpec((tm, D), lambda i: (i, 0)),
            pl.BlockSpec((HALO, D), lambda i: (jnp.maximum(i * (tm // HALO) - 1, 0), 0)),
            pl.BlockSpec((1, D), lambda i: (0, 0)),
            pl.BlockSpec((len(POOL_WINDOWS), POOL_GW, POOL_GW), lambda i: (0, 0, 0)),
            pl.BlockSpec((1, D), lambda i: (0, 0)),
        ],
        out_specs=[
            pl.BlockSpec((tm, D), lambda i: (i, 0)),
            pl.BlockSpec((None, HALO, D), lambda i: (i, 0, 0)),
        ],
        out_shape=[
            jax.ShapeDtypeStruct((P, D), F32),
            jax.ShapeDtypeStruct((P // tm, HALO, D), F32),
        ],
        scratch_shapes=[pltpu.VMEM((tm + HALO, D), F32)],
        compiler_params=_params("arbitrary"),
        name="pool_prompt",
    )(x, x, gain.reshape(1, D), w_pool, scale.reshape(1, D))


HEADS_PER_STEP = 8


def _sattn_body(qt_ref, knt_ref, vnt_ref, k0, v0, k1, v1, k2, v2, bp0, bp1, bp2, b0_ref, o_ref):
    nhl = HEADS_PER_STEP
    us, ms, ls = [], [], []
    for g, (kr, vr, bp) in enumerate(((k0, v0, bp0), (k1, v1, bp1), (k2, v2, bp2))):
        q_all = qt_ref[g] * SCALE
        kn_all = knt_ref[g]
        vn_all = vnt_ref[g]
        w = bp.shape[1]

        def fold(x, op):
            out = x[:, 0:128]
            for t in range(1, w // 128):
                out = op(out, x[:, t * 128:(t + 1) * 128])
            return out

        cols = [q_all[:, hl:hl + 1] for hl in range(nhl)]
        valids, srows = [], []
        for hl in range(nhl):
            bias = bp[hl:hl + 1, :]
            valids.append(bias > 0.5 * NEG)
            raw = jnp.sum(kr[hl] * cols[hl], axis=0, keepdims=True)
            srows.append(jnp.where(valids[hl], raw + bias, NEG))
        s0 = jnp.concatenate([jnp.sum(kn_all[:, hl:hl + 1] * cols[hl], axis=0, keepdims=True)
                              for hl in range(nhl)], axis=0) + b0_ref[g][:, 0:1]
        pm = jnp.concatenate([fold(r, jnp.maximum) for r in srows], axis=0)
        m = jnp.maximum(jnp.max(pm, axis=1, keepdims=True), s0)
        prows = [jnp.exp(srows[hl] - m[hl:hl + 1, :]) for hl in range(nhl)]
        p0 = jnp.exp(s0 - m)
        ps = jnp.concatenate([fold(r, jnp.add) for r in prows], axis=0)
        ls.append(jnp.sum(ps, axis=1, keepdims=True) + p0)
        ms.append(m)
        us.append([jnp.sum(fold(jnp.where(valids[hl], vr[hl] * prows[hl], 0.0), jnp.add), axis=1, keepdims=True)
                   + vn_all[:, hl:hl + 1] * p0[hl:hl + 1, :] for hl in range(nhl)])
    m_all = jnp.maximum(jnp.maximum(ms[0], ms[1]), ms[2])
    es = [jnp.exp(m - m_all) for m in ms]
    l_all = es[0] * ls[0] + es[1] * ls[1] + es[2] * ls[2]
    cs = [e / l_all for e in es]
    lane = lax.broadcasted_iota(jnp.int32, (HEAD_DIM, 128), 1)
    out = jnp.zeros((HEAD_DIM, 128), F32)
    for hl in range(nhl):
        oh = (cs[0][hl:hl + 1, :] * us[0][hl] + cs[1][hl:hl + 1, :] * us[1][hl]
              + cs[2][hl:hl + 1, :] * us[2][hl])
        out = jnp.where(lane == hl, oh, out)
    o_ref[...] = out


def _sample_attn(qkv, caches_t, a, bias_pos, bias_new):
    S = qkv.shape[0]
    nh = N_HEADS // HEADS_PER_STEP
    t = qkv.reshape(S, 3 * N_GROUPS, nh, HEADS_PER_STEP, HEAD_DIM).transpose(0, 1, 2, 4, 3)
    t = jnp.pad(t, ((0, 0),) * 4 + ((0, 128 - HEADS_PER_STEP),))
    t = t.reshape(S, N_GROUPS, 3, nh, HEAD_DIM, 128)
    small = pl.BlockSpec((None, N_GROUPS, None, HEAD_DIM, 128), lambda b, hh: (b, 0, hh, 0, 0))
    in_specs = [small, small, small]
    args = [t[:, :, 0], t[:, :, 1], t[:, :, 2]]
    for g, (ck, cv) in enumerate(caches_t):
        w = WINDOWS[g]
        for c in (ck, cv):
            args.append(c)
            in_specs.append(pl.BlockSpec((None, None, HEADS_PER_STEP, HEAD_DIM, w),
                                         lambda b, hh: (a, b, hh, 0, 0)))
    for g in range(N_GROUPS):
        args.append(bias_pos[g])
        in_specs.append(pl.BlockSpec((HEADS_PER_STEP, WINDOWS[g]), lambda b, hh: (hh, 0)))
    args.append(bias_new)
    in_specs.append(pl.BlockSpec((None, N_GROUPS, HEADS_PER_STEP, 128), lambda b, hh: (hh, 0, 0, 0)))
    out = pl.pallas_call(
        _sattn_body,
        grid=(S, nh),
        in_specs=in_specs,
        out_specs=pl.BlockSpec((None, None, HEAD_DIM, 128), lambda b, hh: (b, hh, 0, 0)),
        out_shape=jax.ShapeDtypeStruct((S, nh, HEAD_DIM, 128), F32),
        compiler_params=_params("arbitrary", "arbitrary"),
        name="sample_attn",
    )(*args)
    return out[..., :HEADS_PER_STEP].transpose(0, 1, 3, 2).reshape(S, D)


def _spool_body(x_ref, st_ref, g_ref, wp_ref, sc_ref, y_ref, h_ref):
    x = x_ref[...]
    h = _rms(x, g_ref[...])
    h_ref[...] = h
    for gi, w in enumerate(POOL_WINDOWS):
        c0 = gi * POOL_GW
        ssum = h[:, c0:c0 + POOL_GW]
        for kk in range(1, w):
            ssum = ssum + st_ref[POOL_STATE - kk, :, c0:c0 + POOL_GW]
        z = ssum / float(w) - h[:, c0:c0 + POOL_GW]
        o = jnp.dot(z.astype(BF16), wp_ref[gi], preferred_element_type=F32)
        y_ref[:, c0:c0 + POOL_GW] = x[:, c0:c0 + POOL_GW] + o * sc_ref[:, c0:c0 + POOL_GW]


def _pool_sample(x, state_t, gain, w_pool, scale):
    S = x.shape[0]
    full = lambda shape: pl.BlockSpec(shape, lambda i: (0,) * len(shape))
    return pl.pallas_call(
        _spool_body,
        grid=(1,),
        in_specs=[full((S, D)), full((POOL_STATE, S, D)), full((1, D)),
                  full((len(POOL_WINDOWS), POOL_GW, POOL_GW)), full((1, D))],
        out_specs=[full((S, D)), full((S, D))],
        out_shape=[jax.ShapeDtypeStruct((S, D), F32), jax.ShapeDtypeStruct((S, D), F32)],
        compiler_params=_params("arbitrary"),
        name="pool_sample",
    )(x, state_t, gain.reshape(1, D), w_pool, scale.reshape(1, D))


def _sffn_body(*refs, final):
    u_ref, st_ref, cw_ref, cb_ref, wo_ref, x_ref = refs[:6]
    refs = refs[6:]
    if final:
        gf_ref = refs[0]
        refs = refs[1:]
    (y_ref,) = refs
    gate = u_ref[:, 0:D_FF]
    val = u_ref[:, D_FF:2 * D_FF]
    conv = cb_ref[...] + st_ref[0] * cw_ref[0:1, :]
    conv = conv + st_ref[1] * cw_ref[1:2, :]
    conv = conv + gate * cw_ref[2:3, :]
    act = (_silu(conv) * val).astype(BF16)
    y = x_ref[...] + jnp.dot(act, wo_ref[...], preferred_element_type=F32)
    if final:
        y = _rms(y, gf_ref[...])
    y_ref[...] = y


def _ffn_sample(u, state_t, conv_w, conv_b, w_out, x, final_gain=None):
    S = x.shape[0]
    final = final_gain is not None
    full = lambda shape: pl.BlockSpec(shape, lambda i: (0,) * len(shape))
    in_specs = [full((S, 2 * D_FF)), full((CONV_WIDTH - 1, S, D_FF)), full((CONV_WIDTH, D_FF)),
                full((1, D_FF)), full((D_FF, D)), full((S, D))]
    args = [u, state_t, conv_w, conv_b.reshape(1, D_FF), w_out, x]
    if final:
        in_specs.append(full((1, D)))
        args.append(final_gain.reshape(1, D))
    return pl.pallas_call(
        functools.partial(_sffn_body, final=final),
        grid=(1,),
        in_specs=in_specs,
        out_specs=full((S, D)),
        out_shape=jax.ShapeDtypeStruct((S, D), F32),
        compiler_params=_params("arbitrary"),
        name="ffn_sample",
    )(*args)


def kernel(x_prompt, x_sample, cache_k_w128, cache_v_w128, cache_k_w512, cache_v_w512, cache_k_w2048, cache_v_w2048, state_pool, state_conv, rel_bias, norm_mix, w_qkv, w_o, w_pool, pool_scale, norm_ffn, w_in, conv_w, conv_b, w_out, norm_final):
    B, T, _ = x_prompt.shape
    S = x_sample.shape[0]
    assert x_sample.shape[1] == 1 and T % (SPAN * DILATIONS[-1]) == 0
    depth = w_in.shape[0]
    P = B * T
    caches = ((cache_k_w128, cache_v_w128), (cache_k_w512, cache_v_w512), (cache_k_w2048, cache_v_w2048))
    for g, (ck, _) in enumerate(caches):
        assert ck.shape[2] == WINDOWS[g]

    wqkv = w_qkv.astype(BF16)
    wo = w_o.astype(BF16)
    wpool = w_pool.astype(BF16)
    win = w_in.astype(BF16)
    wout = w_out.astype(BF16)

    bgs = [_group_bias(rel_bias, g) for g in range(N_GROUPS)]
    bias_p = [_prompt_bias_table(bg) for bg in bgs]
    sample_tabs = [_sample_bias_tables(bg, g) for g, bg in enumerate(bgs)]
    bias_pos = [t[0] for t in sample_tabs]
    nh = N_HEADS // HEADS_PER_STEP
    bias_new = jnp.stack([t[1] for t in sample_tabs]).reshape(N_GROUPS, nh, HEADS_PER_STEP)
    bias_new = jnp.broadcast_to(bias_new.transpose(1, 0, 2)[..., None], (nh, N_GROUPS, HEADS_PER_STEP, 128))
    caches_t = tuple((jnp.transpose(ck, (0, 1, 3, 4, 2)), jnp.transpose(cv, (0, 1, 3, 4, 2)))
                     for ck, cv in caches)

    xp = x_prompt.reshape(P, D)
    xs = x_sample.reshape(S, D)
    nk_p = [[] for _ in range(N_GROUPS)]
    nv_p = [[] for _ in range(N_GROUPS)]
    nk_s = [[] for _ in range(N_GROUPS)]
    nv_s = [[] for _ in range(N_GROUPS)]
    pool_p, pool_s, conv_p, conv_s = [], [], [], []

    for i in range(depth):
        if i % 2 == 0:
            a = i // 2
            state = None
            for g in range(N_GROUPS):
                d = DILATIONS[g]
                qt, k, vt, ktl, vtl = _qkv_stream(xp.reshape(B, T, D), norm_mix[i], wqkv, a, g)
                state = _attn_pass(qt, k, vt, bias_p[g], state, g, B, T)
                nk_p[g].append(ktl.reshape(B, SPAN * d, N_HEADS, HEAD_DIM))
                nv_p[g].append(vtl.reshape(B, SPAN * d, N_HEADS, HEAD_DIM))
            xp = _mm_res(state.reshape(P, D), wo[a], xp)

            qkv_s = _norm_mm(xs, norm_mix[i], wqkv, a, tn=1024)
            o_s = _sample_attn(qkv_s, caches_t, a, bias_pos, bias_new)
            xs = _mm_res(o_s, wo[a], xs)
            for g in range(N_GROUPS):
                c = g * 3 * D
                nk_s[g].append(qkv_s[:, c + D:c + 2 * D].reshape(S, 1, N_HEADS, HEAD_DIM))
                nv_s[g].append(qkv_s[:, c + 2 * D:c + 3 * D].reshape(S, 1, N_HEADS, HEAD_DIM))
        else:
            b = i // 2
            xp, hs = _pool_prompt(xp, T, norm_mix[i], wpool[b], pool_scale[b])
            tiles = hs.shape[0] // B
            pool_p.append(hs.reshape(B, tiles, HALO, D)[:, -1, HALO - POOL_STATE:, :])
            st = state_pool[b]
            xs, h_s = _pool_sample(xs, jnp.swapaxes(st, 0, 1), norm_mix[i], wpool[b], pool_scale[b])
            pool_s.append(jnp.concatenate([st[:, 1:], h_s[:, None, :]], axis=1))

        fg = norm_final if i == depth - 1 else None
        xp, cs = _ffn_prompt(xp, T, norm_ffn[i], win, conv_w[i], conv_b[i], wout, i, fg)
        tiles = cs.shape[0] // B
        conv_p.append(cs.reshape(B, tiles, 8, D_FF)[:, -1, 8 - (CONV_WIDTH - 1):, :])
        u_s = _norm_mm(xs, norm_ffn[i], win, i, tn=512)
        st = state_conv[i]
        xs = _ffn_sample(u_s, jnp.swapaxes(st, 0, 1), conv_w[i], conv_b[i], wout[i], xs, fg)
        conv_s.append(jnp.concatenate([st[:, 1:], u_s[:, None, :D_FF]], axis=1))

    y_prompt = xp.reshape(B, T, D)
    y_sample = xs.reshape(S, 1, D)
    outs = [y_prompt, y_sample]
    for g in range(N_GROUPS):
        outs += [jnp.stack(nk_p[g]), jnp.stack(nv_p[g])]
    for g in range(N_GROUPS):
        outs += [jnp.stack(nk_s[g]), jnp.stack(nv_s[g])]
    outs += [jnp.stack(pool_p), jnp.stack(pool_s), jnp.stack(conv_p), jnp.stack(conv_s)]
    return tuple(outs)
```
